```python
import jax, jax.numpy as jnp
from jax import lax
import numpy as np

D_MODEL = 1024
BATCH = 8
SEQ = 2048
DEPTH = 4

HEAD_DIM = 64
N_HEADS = D_MODEL // HEAD_DIM
CONV_HEADS = N_HEADS // 4
SG_HEADS = N_HEADS // 4
SB_HEADS = N_HEADS // 2
CONV_WIDTH = CONV_HEADS * HEAD_DIM
SG_WIDTH = SG_HEADS * HEAD_DIM
SB_WIDTH = SB_HEADS * HEAD_DIM
MIX_WIDTH = CONV_WIDTH + SG_WIDTH + SB_WIDTH
CONV_KERNEL = 31
SG_CHUNK = 128
SB_BLOCK = 128
OFF_CONV = 0
OFF_SG = OFF_CONV + 2 * CONV_WIDTH
OFF_SB = OFF_SG + 2 * SG_WIDTH
IN_WIDTH = OFF_SB + 3 * SB_WIDTH
FFN_HIDDEN = ((8 * D_MODEL + 3 * 256 - 1) // (3 * 256)) * 256
RMS_EPS = 1e-6
LN_EPS = 1e-5

kernel_name = "hybrid_conv_gmlp_stickbreaking_trunk"


def rms_norm(x, g):
    xf = x.astype(jnp.float32)
    y = xf * lax.rsqrt(jnp.mean(xf * xf, axis=-1, keepdims=True) + RMS_EPS)
    return (y * g.astype(jnp.float32)).astype(x.dtype)


def layer_norm(x, g, b):
    xf = x.astype(jnp.float32)
    mu = jnp.mean(xf, axis=-1, keepdims=True)
    xc = xf - mu
    var = jnp.mean(xc * xc, axis=-1, keepdims=True)
    y = xc * lax.rsqrt(var + LN_EPS) * g.astype(jnp.float32) + b.astype(jnp.float32)
    return y.astype(x.dtype)


def conv_module(val, gate, conv_w, conv_b, ln_g, ln_b):
    h = val * jax.nn.sigmoid(gate)
    h = lax.conv_general_dilated(
        h, conv_w[:, None, :].astype(h.dtype), window_strides=(1,),
        padding=[(CONV_KERNEL - 1, 0)],
        dimension_numbers=('NWC', 'WIO', 'NWC'),
        feature_group_count=CONV_WIDTH) + conv_b
    h = layer_norm(h, ln_g, ln_b)
    return jax.nn.silu(h)


def spatial_gating(uv, ln_g, ln_b, sg_w, sg_b):
    uv = jax.nn.gelu(uv, approximate=False)
    u, v = uv[..., :SG_WIDTH], uv[..., SG_WIDTH:]
    v = layer_norm(v, ln_g, ln_b)
    bsz, seq, _ = v.shape
    n_chunks = seq // SG_CHUNK
    v = v.reshape(bsz, n_chunks, SG_CHUNK, SG_HEADS, HEAD_DIM)
    causal = jnp.tril(jnp.ones((SG_CHUNK, SG_CHUNK), dtype=bool))
    w = jnp.where(causal[None], sg_w, 0)
    mixed = jnp.einsum('hts,bnshd->bnthd', w, v) + sg_b.T[None, None, :, :, None]
    return u * mixed.reshape(bsz, seq, SG_WIDTH)


def stick_breaking_attention(q, k, v):
    scale = HEAD_DIM ** -0.5
    seq = q.shape[2]
    outs = []
    for blk in range(seq // SB_BLOCK):
        t0 = blk * SB_BLOCK
        kv_len = t0 + SB_BLOCK
        qb = q[:, :, t0:kv_len].astype(jnp.float32)
        kb = k[:, :, :kv_len].astype(jnp.float32)
        vb = v[:, :, :kv_len].astype(jnp.float32)
        z = jnp.einsum('bhtd,bhsd->bhts', qb, kb) * scale
        t_pos = t0 + jnp.arange(SB_BLOCK)
        s_pos = jnp.arange(kv_len)
        causal = s_pos[None, :] < t_pos[:, None]
        log_not_beta = jnp.where(causal, jax.nn.log_sigmoid(-z), 0.0)
        between = lax.cumsum(log_not_beta, axis=3, reverse=True) - log_not_beta
        att = jnp.where(causal, jnp.exp(jax.nn.log_sigmoid(z) + between), 0.0)
        outs.append(jnp.einsum('bhts,bhsd->bhtd', att, vb))
    return jnp.concatenate(outs, axis=2).astype(v.dtype)


def hybrid_layer(x, mix_norm_g, w_in, conv_w, conv_b, conv_ln_g, conv_ln_b,
                 sg_ln_g, sg_ln_b, sg_w, sg_b, q_norm_g, k_norm_g, out_norm_g,
                 w_out, ffn_norm_g, w_gate_up, w_down):
    bsz, seq, _ = x.shape
    h = rms_norm(x, mix_norm_g)
    proj = jnp.einsum('bsd,de->bse', h, w_in)

    y_conv = conv_module(proj[..., OFF_CONV:OFF_CONV + CONV_WIDTH],
                         proj[..., OFF_CONV + CONV_WIDTH:OFF_SG],
                         conv_w, conv_b, conv_ln_g, conv_ln_b)

    y_sg = spatial_gating(proj[..., OFF_SG:OFF_SB], sg_ln_g, sg_ln_b, sg_w, sg_b)

    qkv = proj[..., OFF_SB:].reshape(bsz, seq, 3, SB_HEADS, HEAD_DIM)
    q = rms_norm(qkv[:, :, 0], q_norm_g).transpose(0, 2, 1, 3)
    k = rms_norm(qkv[:, :, 1], k_norm_g).transpose(0, 2, 1, 3)
    v = qkv[:, :, 2].transpose(0, 2, 1, 3)
    y_sb = stick_breaking_attention(q, k, v).transpose(0, 2, 1, 3).reshape(bsz, seq, SB_WIDTH)

    y = jnp.concatenate([
        rms_norm(y_conv, out_norm_g[:CONV_WIDTH]),
        rms_norm(y_sg, out_norm_g[CONV_WIDTH:CONV_WIDTH + SG_WIDTH]),
        rms_norm(y_sb, out_norm_g[CONV_WIDTH + SG_WIDTH:]),
    ], axis=-1)
    x = x + jnp.einsum('bse,ed->bsd', y, w_out)

    h = rms_norm(x, ffn_norm_g)
    gu = jnp.einsum('bsd,df->bsf', h, w_gate_up)
    act = jax.nn.silu(gu[..., :FFN_HIDDEN]) * gu[..., FFN_HIDDEN:]
    return x + jnp.einsum('bsf,fd->bsd', act, w_down)


def setup_inputs(seed: int = 0) -> dict:
    key = jax.random.key(seed)
    ks = jax.random.split(key, 20)
    f32 = jnp.float32
    nrm = lambda k, shape: jax.random.normal(k, shape, dtype=f32)
    L = DEPTH
    return {
        "x": nrm(ks[0], (BATCH, SEQ, D_MODEL)),
        "mix_norm_g": 1.0 + 0.02 * nrm(ks[1], (L, D_MODEL)),
        "w_in": nrm(ks[2], (L, D_MODEL, IN_WIDTH)) * D_MODEL ** -0.5,
        "conv_w": nrm(ks[3], (L, CONV_KERNEL, CONV_WIDTH)) * CONV_KERNEL ** -0.5,
        "conv_b": 0.02 * nrm(ks[4], (L, CONV_WIDTH)),
        "conv_ln_g": 1.0 + 0.02 * nrm(ks[5], (L, CONV_WIDTH)),
        "conv_ln_b": 0.02 * nrm(ks[6], (L, CONV_WIDTH)),
        "sg_ln_g": 1.0 + 0.02 * nrm(ks[7], (L, SG_WIDTH)),
        "sg_ln_b": 0.02 * nrm(ks[8], (L, SG_WIDTH)),
        "sg_w": nrm(ks[9], (L, SG_HEADS, SG_CHUNK, SG_CHUNK)) * SG_CHUNK ** -0.5,
        "sg_b": 1.0 + 0.1 * nrm(ks[10], (L, SG_HEADS, SG_CHUNK)),
        "q_norm_g": 1.0 + 0.02 * nrm(ks[11], (L, HEAD_DIM)),
        "k_norm_g": 1.0 + 0.02 * nrm(ks[12], (L, HEAD_DIM)),
        "out_norm_g": 1.0 + 0.02 * nrm(ks[13], (L, MIX_WIDTH)),
        "w_out": nrm(ks[14], (L, MIX_WIDTH, D_MODEL)) * MIX_WIDTH ** -0.5,
        "ffn_norm_g": 1.0 + 0.02 * nrm(ks[15], (L, D_MODEL)),
        "w_gate_up": nrm(ks[16], (L, D_MODEL, 2 * FFN_HIDDEN)) * D_MODEL ** -0.5,
        "w_down": nrm(ks[17], (L, FFN_HIDDEN, D_MODEL)) * FFN_HIDDEN ** -0.5,
    }


def reference(x, mix_norm_g, w_in, conv_w, conv_b, conv_ln_g, conv_ln_b,
              sg_ln_g, sg_ln_b, sg_w, sg_b, q_norm_g, k_norm_g, out_norm_g,
              w_out, ffn_norm_g, w_gate_up, w_down):
    for l in range(DEPTH):
        x = hybrid_layer(x, mix_norm_g[l], w_in[l], conv_w[l], conv_b[l],
                         conv_ln_g[l], conv_ln_b[l], sg_ln_g[l], sg_ln_b[l],
                         sg_w[l], sg_b[l], q_norm_g[l], k_norm_g[l], out_norm_g[l],
                         w_out[l], ffn_norm_g[l], w_gate_up[l], w_down[l])
    return x
```

```python
import functools
import math

import jax
import jax.numpy as jnp
from jax import lax
from jax.experimental import pallas as pl
from jax.experimental.pallas import tpu as pltpu

F32 = jnp.float32
BF16 = jnp.bfloat16

D_MODEL = 1024
SEQ = 2048
DEPTH = 4
HEAD_DIM = 64
CONV_WIDTH = 256
SG_WIDTH = 256
SB_WIDTH = 512
SB_HEADS = SB_WIDTH // HEAD_DIM
SG_HEADS = SG_WIDTH // HEAD_DIM
CONV_KERNEL = 31
SG_CHUNK = 128
OFF_SG = 2 * CONV_WIDTH
OFF_SB = OFF_SG + 2 * SG_WIDTH
IN_WIDTH = OFF_SB + 3 * SB_WIDTH
FFN_HIDDEN = 2816
RMS_EPS = 1e-6
LN_EPS = 1e-5

LANES = 128
HALO = 32
TM_IN = 256
TM_FFN = 256
TQ = 128
TK = 256
VMEM_LIMIT = 56 * 1024 * 1024


def _rms(x, g):
    return x * lax.rsqrt(jnp.mean(x * x, axis=-1, keepdims=True) + RMS_EPS) * g


def _ln(x, g, b):
    mu = jnp.mean(x, axis=-1, keepdims=True)
    xc = x - mu
    var = jnp.mean(xc * xc, axis=-1, keepdims=True)
    return xc * lax.rsqrt(var + LN_EPS) * g + b


def _mix_in_kernel(x_ref, g_ref, w_in_ref, cw_ref, cb_ref, clg_ref, clb_ref,
                   slg_ref, slb_ref, sgw_ref, sgb_ref, qg_ref, kg_ref, og_ref, bd_ref,
                   ya_ref, q_ref, k_ref, v_ref, hist_ref):
    tm = x_ref.shape[0]
    tiles_per_seq = SEQ // tm

    h = _rms(x_ref[...], g_ref[...])
    proj = jnp.dot(h.astype(BF16), w_in_ref[...], preferred_element_type=F32)

    @pl.when(pl.program_id(0) % tiles_per_seq == 0)
    def _():
        hist_ref[0:HALO, :] = jnp.zeros((HALO, CONV_WIDTH), F32)

    hist_ref[HALO:HALO + tm, :] = proj[:, 0:CONV_WIDTH] * jax.nn.sigmoid(proj[:, CONV_WIDTH:OFF_SG])
    first = HALO - (CONV_KERNEL - 1)
    conv = jnp.zeros((tm, CONV_WIDTH), F32) + cb_ref[...]
    for j in range(CONV_KERNEL):
        conv = conv + hist_ref[first + j:first + j + tm, :] * cw_ref[j:j + 1, :]
    hist_ref[0:HALO, :] = hist_ref[tm:tm + HALO, :]
    yc = _ln(conv, clg_ref[...], clb_ref[...])
    yc = yc * jax.nn.sigmoid(yc)
    ya_ref[:, 0:CONV_WIDTH] = _rms(yc, og_ref[:, 0:CONV_WIDTH]).astype(BF16)

    uv = proj[:, OFF_SG:OFF_SB]
    uv = 0.5 * uv * (1.0 + lax.erf(uv * (1.0 / math.sqrt(2.0))))
    u = uv[:, 0:SG_WIDTH]
    vn = _ln(uv[:, SG_WIDTH:], slg_ref[...], slb_ref[...]).astype(BF16)
    row = lax.broadcasted_iota(jnp.int32, (SG_CHUNK, SG_CHUNK), 0)
    col = lax.broadcasted_iota(jnp.int32, (SG_CHUNK, SG_CHUNK), 1)
    lane = lax.broadcasted_iota(jnp.int32, (SG_CHUNK, SG_WIDTH), 1)
    ws = [jnp.where(row >= col, sgw_ref[hd], 0.0).astype(BF16) for hd in range(SG_HEADS)]
    for c in range(tm // SG_CHUNK):
        rows = slice(c * SG_CHUNK, (c + 1) * SG_CHUNK)
        vc = vn[rows, :]
        mixed = jnp.dot(ws[SG_HEADS - 1], vc, preferred_element_type=F32)
        for hd in range(SG_HEADS - 2, -1, -1):
            mixed = jnp.where(lane < (hd + 1) * HEAD_DIM,
                              jnp.dot(ws[hd], vc, preferred_element_type=F32), mixed)
        ysg = u[rows, :] * (mixed + sgb_ref[...])
        ya_ref[rows, CONV_WIDTH:] = _rms(ysg, og_ref[:, CONV_WIDTH:]).astype(BF16)

    def head_norm(t, g):
        ms = jnp.dot((t * t).astype(BF16), bd_ref[...], preferred_element_type=F32)
        return t * lax.rsqrt(ms + RMS_EPS) * g

    q_ref[...] = head_norm(proj[:, OFF_SB:OFF_SB + SB_WIDTH], qg_ref[...]).astype(BF16)
    k_ref[...] = head_norm(proj[:, OFF_SB + SB_WIDTH:OFF_SB + 2 * SB_WIDTH], kg_ref[...]).astype(BF16)
    v_ref[...] = proj[:, OFF_SB + 2 * SB_WIDTH:].astype(BF16)


def _attn_kernel(q_ref, k_ref, v_ref, u_ref, og_ref, o_ref, acc_ref, carry_ref):
    qi = pl.program_id(1)
    jd = qi // (TK // TQ)
    acc_ref[...] = jnp.zeros_like(acc_ref)
    carry_ref[...] = jnp.zeros_like(carry_ref)

    lane = lax.broadcasted_iota(jnp.int32, (TQ, LANES), 1)
    qs = []
    for p in range(SB_HEADS // 2):
        qp = q_ref[0, :, p * LANES:(p + 1) * LANES]
        qs.append(jnp.where(lane < HEAD_DIM, qp, jnp.zeros_like(qp)))
        qs.append(jnp.where(lane >= HEAD_DIM, qp, jnp.zeros_like(qp)))

    def block(j, masked):
        k0 = pl.multiple_of(j * TK, TK)
        if masked:
            t_pos = qi * TQ + lax.broadcasted_iota(jnp.int32, (TQ, TK), 0)
            s_pos = j * TK + lax.broadcasted_iota(jnp.int32, (TQ, TK), 1)
            causal = s_pos < t_pos
        for p in range(SB_HEADS // 2):
            kp = k_ref[0, pl.ds(k0, TK), p * LANES:(p + 1) * LANES]
            vp = v_ref[0, pl.ds(k0, TK), p * LANES:(p + 1) * LANES]
            for e in range(2):
                hd = 2 * p + e
                z = lax.dot_general(qs[hd], kp, (((1,), (1,)), ((), ())),
                                    preferred_element_type=F32)
                sp = jnp.maximum(z, 0.0) + jnp.log(1.0 + jnp.exp(-jnp.abs(z)))
                if masked:
                    sp = jnp.where(causal, sp, 0.0)
                later = jnp.dot(sp.astype(BF16), u_ref[...], preferred_element_type=F32)
                carry = carry_ref[hd]
                att = jnp.exp((z - sp) - later - jnp.concatenate([carry] * (TK // LANES), axis=1))
                if masked:
                    att = jnp.where(causal, att, 0.0)
                acc_ref[hd] += jnp.dot(att.astype(BF16), vp, preferred_element_type=F32)
                carry_ref[hd] = carry + jnp.sum(sp, axis=1, keepdims=True)

    block(jd, True)

    def body(i, c):
        block(jd - 1 - i, False)
        return c

    lax.fori_loop(0, jd, body, 0)

    y = jnp.concatenate(
        [jnp.where(lane < HEAD_DIM, acc_ref[2 * p], acc_ref[2 * p + 1]) for p in range(SB_HEADS // 2)], axis=1)
    o_ref[0] = _rms(y, og_ref[...]).astype(BF16)


def _out_ffn_kernel(x_ref, ya_ref, ysb_ref, wout_ref, fg_ref, wgu_ref, wd_ref, o_ref):
    half = ya_ref.shape[1]
    x1 = (x_ref[...]
          + jnp.dot(ya_ref[...], wout_ref[0:half, :], preferred_element_type=F32)
          + jnp.dot(ysb_ref[...], wout_ref[half:, :], preferred_element_type=F32))
    h = _rms(x1, fg_ref[...]).astype(BF16)
    gu = jnp.dot(h, wgu_ref[...], preferred_element_type=F32)
    gate = gu[:, 0:FFN_HIDDEN]
    act = (gate * jax.nn.sigmoid(gate) * gu[:, FFN_HIDDEN:]).astype(BF16)
    o_ref[...] = x1 + jnp.dot(act, wd_ref[...], preferred_element_type=F32)


def _const_spec(shape):
    nd = len(shape)
    return pl.BlockSpec(shape, lambda *_: (0,) * nd, pipeline_mode=pl.Buffered(1))


def _mix_in(x2, g, w_in, cw, cb, clg, clb, slg, slb, sgw, sgb, qg, kg, og, bd):
    n = x2.shape[0]
    tm = TM_IN
    row = lambda w: pl.BlockSpec((tm, w), lambda i: (i, 0))
    consts = [g, w_in, cw, cb, clg, clb, slg, slb, sgw, sgb, qg, kg, og, bd]
    return pl.pallas_call(
        _mix_in_kernel,
        grid=(n // tm,),
        in_specs=[row(D_MODEL)] + [_const_spec(c.shape) for c in consts],
        out_specs=[row(CONV_WIDTH + SG_WIDTH), row(SB_WIDTH), row(SB_WIDTH), row(SB_WIDTH)],
        out_shape=[jax.ShapeDtypeStruct((n, CONV_WIDTH + SG_WIDTH), BF16)]
        + [jax.ShapeDtypeStruct((n, SB_WIDTH), BF16)] * 3,
        scratch_shapes=[pltpu.VMEM((HALO + tm, CONV_WIDTH), F32)],
        compiler_params=pltpu.CompilerParams(
            dimension_semantics=("arbitrary",), vmem_limit_bytes=VMEM_LIMIT),
        name="mix_in",
    )(x2, *consts)


def _attn(q, k, v, u, og):
    b, s, w = q.shape
    return pl.pallas_call(
        _attn_kernel,
        grid=(b, s // TQ),
        in_specs=[pl.BlockSpec((1, TQ, w), lambda bi, qi: (bi, qi, 0)),
                  pl.BlockSpec((1, s, w), lambda bi, qi: (bi, 0, 0)),
                  pl.BlockSpec((1, s, w), lambda bi, qi: (bi, 0, 0)),
                  _const_spec(u.shape), _const_spec(og.shape)],
        out_specs=pl.BlockSpec((1, TQ, w), lambda bi, qi: (bi, qi, 0)),
        out_shape=jax.ShapeDtypeStruct((b, s, w), BF16),
        scratch_shapes=[pltpu.VMEM((SB_HEADS, TQ, LANES), F32),
                        pltpu.VMEM((SB_HEADS, TQ, LANES), F32)],
        compiler_params=pltpu.CompilerParams(
            dimension_semantics=("parallel", "arbitrary"), vmem_limit_bytes=VMEM_LIMIT),
        name="sb_attn",
    )(q, k, v, u, og)


def _out_ffn(x2, ya, ysb, wout, fg, wgu, wd):
    n = x2.shape[0]
    tm = TM_FFN
    row = lambda w: pl.BlockSpec((tm, w), lambda i: (i, 0))
    consts = [wout, fg, wgu, wd]
    return pl.pallas_call(
        _out_ffn_kernel,
        grid=(n // tm,),
        in_specs=[row(D_MODEL), row(ya.shape[1]), row(ysb.shape[1])] + [_const_spec(c.shape) for c in consts],
        out_specs=row(D_MODEL),
        out_shape=jax.ShapeDtypeStruct((n, D_MODEL), F32),
        compiler_params=pltpu.CompilerParams(
            dimension_semantics=("parallel",), vmem_limit_bytes=VMEM_LIMIT),
        name="out_ffn",
    )(x2, ya, ysb, *consts)


def kernel(x, mix_norm_g, w_in, conv_w, conv_b, conv_ln_g, conv_ln_b, sg_ln_g, sg_ln_b, sg_w, sg_b,
           q_norm_g, k_norm_g, out_norm_g, w_out, ffn_norm_g, w_gate_up, w_down):
    bsz, seq, d = x.shape
    assert (seq, d) == (SEQ, D_MODEL) and seq % TM_IN == 0 and seq % TK == 0
    n = bsz * seq
    x2 = x.reshape(n, d)

    hid = jnp.arange(SB_WIDTH) // HEAD_DIM
    bd = jnp.where(hid[:, None] == hid[None, :], 1.0 / HEAD_DIM, 0.0).astype(BF16)
    ki = jnp.arange(TK)
    u = (ki[:, None] > ki[None, :]).astype(BF16)
    row2 = lambda a: a.reshape(1, -1)

    for l in range(DEPTH):
        cw = jnp.pad(conv_w[l], ((0, HALO - CONV_KERNEL), (0, 0)))
        sgb = jnp.repeat(sg_b[l].T, HEAD_DIM, axis=1)
        qg = jnp.tile(q_norm_g[l], SB_HEADS).reshape(1, -1) * (HEAD_DIM ** -0.5)
        kg = jnp.tile(k_norm_g[l], SB_HEADS).reshape(1, -1)
        og = row2(out_norm_g[l])
        ya, q, k, v = _mix_in(
            x2, row2(mix_norm_g[l]), w_in[l].astype(BF16), cw, row2(conv_b[l]),
            row2(conv_ln_g[l]), row2(conv_ln_b[l]), row2(sg_ln_g[l]), row2(sg_ln_b[l]),
            sg_w[l], sgb, qg, kg, og[:, :CONV_WIDTH + SG_WIDTH], bd)
        shp = (bsz, seq, SB_WIDTH)
        ysb = _attn(q.reshape(shp), k.reshape(shp), v.reshape(shp), u, og[:, CONV_WIDTH + SG_WIDTH:])
        x2 = _out_ffn(x2, ya, ysb.reshape(n, SB_WIDTH), w_out[l].astype(BF16), row2(ffn_norm_g[l]),
                      w_gate_up[l].astype(BF16), w_down[l].astype(BF16))
    return x2.reshape(bsz, seq, d)
```

```python
import functools
import math

import jax
import jax.numpy as jnp
from jax import lax
from jax.experimental import pallas as pl
from jax.experimental.pallas import tpu as pltpu

F32 = jnp.float32
BF16 = jnp.bfloat16

D_MODEL = 1024
SEQ = 2048
DEPTH = 4
HEAD_DIM = 64
CONV_WIDTH = 256
SG_WIDTH = 256
SB_WIDTH = 512
SB_HEADS = SB_WIDTH // HEAD_DIM
SG_HEADS = SG_WIDTH // HEAD_DIM
CONV_KERNEL = 31
SG_CHUNK = 128
OFF_SG = 2 * CONV_WIDTH
OFF_SB = OFF_SG + 2 * SG_WIDTH
IN_WIDTH = OFF_SB + 3 * SB_WIDTH
FFN_HIDDEN = 2816
RMS_EPS = 1e-6
LN_EPS = 1e-5
LOG2E = math.log2(math.e)

LANES = 128
SUBLANES = 8
HALO = 32
TM_IN = 512
TM_FFN = 512
TQ = 256
TK = 256
VMEM_LIMIT = 56 * 1024 * 1024


def _rms(x, g):
    return x * lax.rsqrt(jnp.mean(x * x, axis=-1, keepdims=True) + RMS_EPS) * g


def _ln(x, g, b):
    mu = jnp.mean(x, axis=-1, keepdims=True)
    xc = x - mu
    var = jnp.mean(xc * xc, axis=-1, keepdims=True)
    return xc * lax.rsqrt(var + LN_EPS) * g + b


def _mix_in_kernel(x_ref, g_ref, w_in_ref, cw_ref, cb_ref, clg_ref, clb_ref,
                   slg_ref, slb_ref, sgw_ref, sgb_ref, qg_ref, kg_ref, og_ref, bd_ref,
                   ya_ref, q_ref, k_ref, v_ref, hist_ref):
    tm = x_ref.shape[0]
    tiles_per_seq = SEQ // tm

    h = _rms(x_ref[...], g_ref[...])
    proj = jnp.dot(h.astype(BF16), w_in_ref[...], preferred_element_type=F32)

    @pl.when(pl.program_id(0) % tiles_per_seq == 0)
    def _():
        hist_ref[0:HALO, :] = jnp.zeros((HALO, CONV_WIDTH), F32)

    hist_ref[HALO:HALO + tm, :] = proj[:, 0:CONV_WIDTH] * jax.nn.sigmoid(proj[:, CONV_WIDTH:OFF_SG])
    first = HALO - (CONV_KERNEL - 1)
    win = SG_CHUNK + HALO
    for c in range(tm // SG_CHUNK):
        r0 = c * SG_CHUNK
        window = hist_ref[r0:r0 + win, :]
        conv = jnp.zeros((SG_CHUNK, CONV_WIDTH), F32) + cb_ref[...]
        for b in range(SUBLANES):
            taps = [j for j in range(CONV_KERNEL) if (first + j) % SUBLANES == b]
            sh = window if b == 0 else pltpu.roll(window, win - b, 0)
            for j in taps:
                a0 = first + j - b
                conv = conv + sh[a0:a0 + SG_CHUNK, :] * cw_ref[j:j + 1, :]
        yc = _ln(conv, clg_ref[...], clb_ref[...])
        yc = yc * jax.nn.sigmoid(yc)
        ya_ref[r0:r0 + SG_CHUNK, 0:CONV_WIDTH] = _rms(yc, og_ref[:, 0:CONV_WIDTH]).astype(BF16)
    hist_ref[0:HALO, :] = hist_ref[tm:tm + HALO, :]

    uv = proj[:, OFF_SG:OFF_SB]
    uv = 0.5 * uv * (1.0 + lax.erf(uv * (1.0 / math.sqrt(2.0))))
    u = uv[:, 0:SG_WIDTH]
    vn = _ln(uv[:, SG_WIDTH:], slg_ref[...], slb_ref[...]).astype(BF16)
    row = lax.broadcasted_iota(jnp.int32, (SG_CHUNK, SG_CHUNK), 0)
    col = lax.broadcasted_iota(jnp.int32, (SG_CHUNK, SG_CHUNK), 1)
    lane = lax.broadcasted_iota(jnp.int32, (SG_CHUNK, SG_WIDTH), 1)
    ws = [jnp.where(row >= col, sgw_ref[hd], 0.0).astype(BF16) for hd in range(SG_HEADS)]
    for c in range(tm // SG_CHUNK):
        rows = slice(c * SG_CHUNK, (c + 1) * SG_CHUNK)
        vc = vn[rows, :]
        mixed = jnp.dot(ws[SG_HEADS - 1], vc, preferred_element_type=F32)
        for hd in range(SG_HEADS - 2, -1, -1):
            mixed = jnp.where(lane < (hd + 1) * HEAD_DIM,
                              jnp.dot(ws[hd], vc, preferred_element_type=F32), mixed)
        ysg = u[rows, :] * (mixed + sgb_ref[...])
        ya_ref[rows, CONV_WIDTH:] = _rms(ysg, og_ref[:, CONV_WIDTH:]).astype(BF16)

    def head_norm(t, g):
        ms = jnp.dot((t * t).astype(BF16), bd_ref[...], preferred_element_type=F32)
        return t * lax.rsqrt(ms + RMS_EPS) * g

    q_ref[...] = head_norm(proj[:, OFF_SB:OFF_SB + SB_WIDTH], qg_ref[...]).astype(BF16)
    k_ref[...] = head_norm(proj[:, OFF_SB + SB_WIDTH:OFF_SB + 2 * SB_WIDTH], kg_ref[...]).astype(BF16)
    v_ref[...] = proj[:, OFF_SB + 2 * SB_WIDTH:].astype(BF16)


def _attn_kernel(q_ref, k_ref, v_ref, u_ref, og_ref, o_ref, acc_ref, carry_ref, att_ref):
    n_pairs = SB_HEADS // 2
    qi = pl.program_id(1)
    jd = qi // (TK // TQ)
    acc_ref[...] = jnp.zeros_like(acc_ref)
    carry_ref[...] = jnp.zeros_like(carry_ref)

    lane = lax.broadcasted_iota(jnp.int32, (TQ, LANES), 1)
    qs = []
    for p in range(n_pairs):
        qp = q_ref[0, :, p * LANES:(p + 1) * LANES]
        qs.append(jnp.concatenate([jnp.where(lane < HEAD_DIM, qp, jnp.zeros_like(qp)),
                                   jnp.where(lane >= HEAD_DIM, qp, jnp.zeros_like(qp))], axis=0))

    def weighted_values(j):
        k0 = pl.multiple_of(j * TK, TK)
        for p in range(n_pairs):
            acc_ref[p] += jnp.dot(att_ref[p], v_ref[0, pl.ds(k0, TK), p * LANES:(p + 1) * LANES],
                                  preferred_element_type=F32)

    def weights(j, masked, pending=None):
        k0 = pl.multiple_of(j * TK, TK)
        if masked:
            t_pos = qi * TQ + lax.broadcasted_iota(jnp.int32, (2 * TQ, TK), 0) % TQ
            s_pos = j * TK + lax.broadcasted_iota(jnp.int32, (2 * TQ, TK), 1)
            causal = s_pos < t_pos
        zs = [lax.dot_general(qs[p], k_ref[0, pl.ds(k0, TK), p * LANES:(p + 1) * LANES],
                              (((1,), (1,)), ((), ())), preferred_element_type=F32)
              for p in range(n_pairs)]
        if pending is not None:
            weighted_values(pending)
        sps = []
        for p in range(n_pairs):
            z = zs[p]
            sp = jnp.maximum(z, 0.0) + jnp.log(1.0 + jnp.exp2(jnp.abs(z) * -LOG2E))
            if masked:
                sp = jnp.where(causal, sp, 0.0)
            sps.append(sp.astype(BF16))
        half = n_pairs // 2
        later = [jnp.dot(jnp.concatenate(sps[g * half:(g + 1) * half], axis=0), u_ref[...],
                         preferred_element_type=F32) for g in range(2)]
        for p in range(n_pairs):
            lat = later[p // half][(p % half) * 2 * TQ:(p % half + 1) * 2 * TQ, :]
            carry = carry_ref[p]
            att = jnp.exp(zs[p] - lat - jnp.concatenate([carry] * (TK // LANES), axis=1))
            if masked:
                att = jnp.where(causal, att, 0.0)
            att_ref[p] = att.astype(BF16)
            carry_ref[p] = carry + lat[:, 0:1]

    weights(jd, True)

    def body(i, c):
        j = jd - 1 - i
        weights(j, False, pending=j + 1)
        return c

    lax.fori_loop(0, jd, body, 0)
    weighted_values(0)

    y = jnp.concatenate(
        [jnp.where(lane < HEAD_DIM, acc_ref[p, 0:TQ, :], acc_ref[p, TQ:, :]) for p in range(n_pairs)], axis=1)
    o_ref[0] = _rms(y, og_ref[...]).astype(BF16)


def _out_ffn_kernel(x_ref, ya_ref, ysb_ref, wout_ref, fg_ref, wgu_ref, wd_ref, o_ref):
    half = ya_ref.shape[1]
    x1 = (x_ref[...]
          + jnp.dot(ya_ref[...], wout_ref[0:half, :], preferred_element_type=F32)
          + jnp.dot(ysb_ref[...], wout_ref[half:, :], preferred_element_type=F32))
    h = _rms(x1, fg_ref[...]).astype(BF16)
    gu = jnp.dot(h, wgu_ref[...], preferred_element_type=F32)
    gate = gu[:, 0:FFN_HIDDEN]
    act = (gate * jax.nn.sigmoid(gate) * gu[:, FFN_HIDDEN:]).astype(BF16)
    o_ref[...] = x1 + jnp.dot(act, wd_ref[...], preferred_element_type=F32)


def _const_spec(shape):
    nd = len(shape)
    return pl.BlockSpec(shape, lambda *_: (0,) * nd, pipeline_mode=pl.Buffered(1))


def _mix_in(x2, g, w_in, cw, cb, clg, clb, slg, slb, sgw, sgb, qg, kg, og, bd):
    n = x2.shape[0]
    tm = TM_IN
    row = lambda w: pl.BlockSpec((tm, w), lambda i: (i, 0))
    consts = [g, w_in, cw, cb, clg, clb, slg, slb, sgw, sgb, qg, kg, og, bd]
    return pl.pallas_call(
        _mix_in_kernel,
        grid=(n // tm,),
        in_specs=[row(D_MODEL)] + [_const_spec(c.shape) for c in consts],
        out_specs=[row(CONV_WIDTH + SG_WIDTH), row(SB_WIDTH), row(SB_WIDTH), row(SB_WIDTH)],
        out_shape=[jax.ShapeDtypeStruct((n, CONV_WIDTH + SG_WIDTH), BF16)]
        + [jax.ShapeDtypeStruct((n, SB_WIDTH), BF16)] * 3,
        scratch_shapes=[pltpu.VMEM((HALO + tm, CONV_WIDTH), F32)],
        compiler_params=pltpu.CompilerParams(
            dimension_semantics=("arbitrary",), vmem_limit_bytes=VMEM_LIMIT),
        name="mix_in",
    )(x2, *consts)


def _attn(q, k, v, u, og):
    b, s, w = q.shape
    return pl.pallas_call(
        _attn_kernel,
        grid=(b, s // TQ),
        in_specs=[pl.BlockSpec((1, TQ, w), lambda bi, qi: (bi, qi, 0)),
                  pl.BlockSpec((1, s, w), lambda bi, qi: (bi, 0, 0)),
                  pl.BlockSpec((1, s, w), lambda bi, qi: (bi, 0, 0)),
                  _const_spec(u.shape), _const_spec(og.shape)],
        out_specs=pl.BlockSpec((1, TQ, w), lambda bi, qi: (bi, qi, 0)),
        out_shape=jax.ShapeDtypeStruct((b, s, w), BF16),
        scratch_shapes=[pltpu.VMEM((SB_HEADS // 2, 2 * TQ, LANES), F32),
                        pltpu.VMEM((SB_HEADS // 2, 2 * TQ, LANES), F32),
                        pltpu.VMEM((SB_HEADS // 2, 2 * TQ, TK), BF16)],
        compiler_params=pltpu.CompilerParams(
            dimension_semantics=("parallel", "arbitrary"), vmem_limit_bytes=VMEM_LIMIT),
        name="sb_attn",
    )(q, k, v, u, og)


def _out_ffn(x2, ya, ysb, wout, fg, wgu, wd):
    n = x2.shape[0]
    tm = TM_FFN
    row = lambda w: pl.BlockSpec((tm, w), lambda i: (i, 0))
    consts = [wout, fg, wgu, wd]
    return pl.pallas_call(
        _out_ffn_kernel,
        grid=(n // tm,),
        in_specs=[row(D_MODEL), row(ya.shape[1]), row(ysb.shape[1])] + [_const_spec(c.shape) for c in consts],
        out_specs=row(D_MODEL),
        out_shape=jax.ShapeDtypeStruct((n, D_MODEL), F32),
        compiler_params=pltpu.CompilerParams(
            dimension_semantics=("parallel",), vmem_limit_bytes=VMEM_LIMIT),
        name="out_ffn",
    )(x2, ya, ysb, *consts)


def kernel(x, mix_norm_g, w_in, conv_w, conv_b, conv_ln_g, conv_ln_b, sg_ln_g, sg_ln_b, sg_w, sg_b,
           q_norm_g, k_norm_g, out_norm_g, w_out, ffn_norm_g, w_gate_up, w_down):
    bsz, seq, d = x.shape
    assert (seq, d) == (SEQ, D_MODEL) and seq % TM_IN == 0 and seq % TK == 0
    n = bsz * seq
    x2 = x.reshape(n, d)

    hid = jnp.arange(SB_WIDTH) // HEAD_DIM
    bd = jnp.where(hid[:, None] == hid[None, :], 1.0 / HEAD_DIM, 0.0).astype(BF16)
    ki = jnp.arange(TK)
    u = (ki[:, None] >= ki[None, :]).astype(BF16)
    row2 = lambda a: a.reshape(1, -1)

    for l in range(DEPTH):
        cw = jnp.pad(conv_w[l], ((0, HALO - CONV_KERNEL), (0, 0)))
        sgb = jnp.repeat(sg_b[l].T, HEAD_DIM, axis=1)
        qg = jnp.tile(q_norm_g[l], SB_HEADS).reshape(1, -1) * (HEAD_DIM ** -0.5)
        kg = jnp.tile(k_norm_g[l], SB_HEADS).reshape(1, -1)
        og = row2(out_norm_g[l])
        ya, q, k, v = _mix_in(
            x2, row2(mix_norm_g[l]), w_in[l].astype(BF16), cw, row2(conv_b[l]),
            row2(conv_ln_g[l]), row2(conv_ln_b[l]), row2(sg_ln_g[l]), row2(sg_ln_b[l]),
            sg_w[l], sgb, qg, kg, og[:, :CONV_WIDTH + SG_WIDTH], bd)
        shp = (bsz, seq, SB_WIDTH)
        ysb = _attn(q.reshape(shp), k.reshape(shp), v.reshape(shp), u, og[:, CONV_WIDTH + SG_WIDTH:])
        x2 = _out_ffn(x2, ya, ysb.reshape(n, SB_WIDTH), w_out[l].astype(BF16), row2(ffn_norm_g[l]),
                      w_gate_up[l].astype(BF16), w_down[l].astype(BF16))
    return x2.reshape(bsz, seq, d)
```

```python
import math

import jax
import jax.numpy as jnp
from jax import lax
from jax.experimental import pallas as pl
from jax.experimental.pallas import tpu as pltpu

F32 = jnp.float32
BF16 = jnp.bfloat16

D_MODEL = 1024
SEQ = 2048
DEPTH = 4
HEAD_DIM = 64
CONV_WIDTH = 256
SG_WIDTH = 256
SB_WIDTH = 512
SB_HEADS = SB_WIDTH // HEAD_DIM
SG_HEADS = SG_WIDTH // HEAD_DIM
CONV_KERNEL = 31
SG_CHUNK = 128
OFF_SG = 2 * CONV_WIDTH
OFF_SB = OFF_SG + 2 * SG_WIDTH
IN_WIDTH = OFF_SB + 3 * SB_WIDTH
FFN_HIDDEN = 2816
RMS_EPS = 1e-6
LN_EPS = 1e-5
LOG2E = math.log2(math.e)

LANES = 128
SUBLANES = 8
HALO = 32
TM_IN = 1024
TM_FFN = 512
TQ = 256
TK = 256
VMEM_LIMIT = 56 * 1024 * 1024


def _rms(x, g):
    return x * lax.rsqrt(jnp.mean(x * x, axis=-1, keepdims=True) + RMS_EPS) * g


def _ln(x, g, b):
    mu = jnp.mean(x, axis=-1, keepdims=True)
    xc = x - mu
    var = jnp.mean(xc * xc, axis=-1, keepdims=True)
    return xc * lax.rsqrt(var + LN_EPS) * g + b


def _mix_in_kernel(x_ref, g_ref, w_in_ref, cw_ref, cb_ref, clg_ref, clb_ref,
                   slg_ref, slb_ref, sgw_ref, sgb_ref, qg_ref, kg_ref, og_ref, bd_ref,
                   ya_ref, q_ref, k_ref, v_ref, hist_ref):
    tm = x_ref.shape[0]
    tiles_per_seq = SEQ // tm
    n_chunks = tm // SG_CHUNK

    @pl.when(pl.program_id(0) % tiles_per_seq == 0)
    def _():
        hist_ref[0:HALO, :] = jnp.zeros((HALO, CONV_WIDTH), F32)

    h = _rms(x_ref[...], g_ref[...]).astype(BF16)

    def project(lo, hi):
        return jnp.dot(h, w_in_ref[:, lo:hi], preferred_element_type=F32)

    pa = project(0, OFF_SG)
    hist_ref[HALO:HALO + tm, :] = pa[:, 0:CONV_WIDTH] * jax.nn.sigmoid(pa[:, CONV_WIDTH:])
    pb = project(OFF_SG, OFF_SB)
    first = HALO - (CONV_KERNEL - 1)
    win = SG_CHUNK + HALO
    for c in range(n_chunks):
        r0 = c * SG_CHUNK
        window = hist_ref[r0:r0 + win, :]
        conv = jnp.zeros((SG_CHUNK, CONV_WIDTH), F32) + cb_ref[...]
        for b in range(SUBLANES):
            taps = [j for j in range(CONV_KERNEL) if (first + j) % SUBLANES == b]
            sh = window if b == 0 else pltpu.roll(window, win - b, 0)
            for j in taps:
                a0 = first + j - b
                conv = conv + sh[a0:a0 + SG_CHUNK, :] * cw_ref[j:j + 1, :]
        yc = _ln(conv, clg_ref[...], clb_ref[...])
        yc = yc * jax.nn.sigmoid(yc)
        ya_ref[r0:r0 + SG_CHUNK, 0:CONV_WIDTH] = _rms(yc, og_ref[:, 0:CONV_WIDTH]).astype(BF16)
    hist_ref[0:HALO, :] = hist_ref[tm:tm + HALO, :]

    pc = project(OFF_SB, IN_WIDTH)
    row = lax.broadcasted_iota(jnp.int32, (SG_CHUNK, SG_CHUNK), 0)
    col = lax.broadcasted_iota(jnp.int32, (SG_CHUNK, SG_CHUNK), 1)
    lane = lax.broadcasted_iota(jnp.int32, (SG_CHUNK, SG_WIDTH), 1)
    ws = [jnp.where(row >= col, sgw_ref[hd], 0.0).astype(BF16) for hd in range(SG_HEADS)]
    for c in range(n_chunks):
        rows = slice(c * SG_CHUNK, (c + 1) * SG_CHUNK)
        uv = pb[rows, :]
        uv = 0.5 * uv * (1.0 + lax.erf(uv * (1.0 / math.sqrt(2.0))))
        vc = _ln(uv[:, SG_WIDTH:], slg_ref[...], slb_ref[...]).astype(BF16)
        mixed = jnp.dot(ws[SG_HEADS - 1], vc, preferred_element_type=F32)
        for hd in range(SG_HEADS - 2, -1, -1):
            mixed = jnp.where(lane < (hd + 1) * HEAD_DIM,
                              jnp.dot(ws[hd], vc, preferred_element_type=F32), mixed)
        ysg = uv[:, 0:SG_WIDTH] * (mixed + sgb_ref[...])
        ya_ref[rows, CONV_WIDTH:] = _rms(ysg, og_ref[:, CONV_WIDTH:]).astype(BF16)

    def head_norm(t, g):
        ms = jnp.dot((t * t).astype(BF16), bd_ref[...], preferred_element_type=F32)
        return t * lax.rsqrt(ms + RMS_EPS) * g

    q_ref[...] = head_norm(pc[:, 0:SB_WIDTH], qg_ref[...]).astype(BF16)
    k_ref[...] = head_norm(pc[:, SB_WIDTH:2 * SB_WIDTH], kg_ref[...]).astype(BF16)
    v_ref[...] = pc[:, 2 * SB_WIDTH:].astype(BF16)


def _attn_kernel(q_ref, k_ref, v_ref, u_ref, og_ref, o_ref, acc_ref, carry_ref, att_ref, z_ref):
    n_pairs = SB_HEADS // 2
    qi = pl.program_id(1)
    jd = qi // (TK // TQ)
    acc_ref[...] = jnp.zeros_like(acc_ref)
    carry_ref[...] = jnp.zeros_like(carry_ref)

    lane = lax.broadcasted_iota(jnp.int32, (TQ, LANES), 1)
    qs = []
    for p in range(n_pairs):
        qp = q_ref[0, :, p * LANES:(p + 1) * LANES]
        qs.append(jnp.concatenate([jnp.where(lane < HEAD_DIM, qp, jnp.zeros_like(qp)),
                                   jnp.where(lane >= HEAD_DIM, qp, jnp.zeros_like(qp))], axis=0))

    half = n_pairs // 2

    def step(j, slot, masked, pending, prefetch):
        if masked:
            t_pos = qi * TQ + lax.broadcasted_iota(jnp.int32, (2 * TQ, TK), 0) % TQ
            s_pos = j * TK + lax.broadcasted_iota(jnp.int32, (2 * TQ, TK), 1)
            causal = s_pos < t_pos

        def value_dot(p):
            k0 = pl.multiple_of(pending * TK, TK)
            return jnp.dot(att_ref[p], v_ref[0, pl.ds(k0, TK), p * LANES:(p + 1) * LANES],
                           preferred_element_type=F32)

        def score_dot(p):
            k0 = pl.multiple_of(prefetch * TK, TK)
            return lax.dot_general(qs[p], k_ref[0, pl.ds(k0, TK), p * LANES:(p + 1) * LANES],
                                   (((1,), (1,)), ((), ())), preferred_element_type=F32)

        def softplus(p):
            z = z_ref[slot, p]
            sp = jnp.maximum(z, 0.0) + jnp.log(1.0 + jnp.exp2(jnp.abs(z) * -LOG2E))
            if masked:
                sp = jnp.where(causal, sp, 0.0)
            return sp.astype(BF16)

        def later_dot(sps):
            return jnp.dot(jnp.concatenate(sps, axis=0), u_ref[...], preferred_element_type=F32)

        def weights(p, later):
            lat = later[(p % half) * 2 * TQ:(p % half + 1) * 2 * TQ, :]
            carry = carry_ref[p]
            att = jnp.exp(z_ref[slot, p] - lat - jnp.concatenate([carry] * (TK // LANES), axis=1))
            if masked:
                att = jnp.where(causal, att, 0.0)
            att_ref[p] = att.astype(BF16)
            carry_ref[p] = carry + lat[:, 0:1]

        groups = [list(range(g * half, (g + 1) * half)) for g in range(2)]
        vals = {}
        if pending is not None:
            for p in groups[0]:
                vals[p] = value_dot(p)
        sps0 = [softplus(p) for p in groups[0]]
        later0 = later_dot(sps0)
        if pending is not None:
            for p in groups[1]:
                vals[p] = value_dot(p)
            for p in groups[0]:
                acc_ref[p] += vals[p]
        sps1 = [softplus(p) for p in groups[1]]
        later1 = later_dot(sps1)
        nxt = {}
        if prefetch is not None:
            for p in groups[0]:
                nxt[p] = score_dot(p)
        if pending is not None:
            for p in groups[1]:
                acc_ref[p] += vals[p]
        for p in groups[0]:
            weights(p, later0)
        if prefetch is not None:
            for p in groups[0]:
                z_ref[1 - slot, p] = nxt[p]
            for p in groups[1]:
                nxt[p] = score_dot(p)
        for p in groups[1]:
            weights(p, later1)
        if prefetch is not None:
            for p in groups[1]:
                z_ref[1 - slot, p] = nxt[p]

    k0 = pl.multiple_of(jd * TK, TK)
    for p in range(n_pairs):
        z_ref[0, p] = lax.dot_general(qs[p], k_ref[0, pl.ds(k0, TK), p * LANES:(p + 1) * LANES],
                                      (((1,), (1,)), ((), ())), preferred_element_type=F32)
    step(jd, 0, True, None, jnp.maximum(jd - 1, 0))

    def two_steps(i, c):
        j = jd - 1 - 2 * i
        step(j, 1, False, j + 1, j - 1)
        step(j - 1, 0, False, j, jnp.maximum(j - 2, 0))
        return c

    lax.fori_loop(0, jd // 2, two_steps, 0)

    @pl.when(jd % 2 == 1)
    def _():
        step(0, 1, False, 1, None)

    for p in range(n_pairs):
        acc_ref[p] += jnp.dot(att_ref[p], v_ref[0, 0:TK, p * LANES:(p + 1) * LANES],
                              preferred_element_type=F32)

    y = jnp.concatenate(
        [jnp.where(lane < HEAD_DIM, acc_ref[p, 0:TQ, :], acc_ref[p, TQ:, :]) for p in range(n_pairs)], axis=1)
    o_ref[0] = _rms(y, og_ref[...]).astype(BF16)


def _out_ffn_kernel(x_ref, ya_ref, ysb_ref, wout_ref, fg_ref, wgu_ref, wd_ref, o_ref):
    half = ya_ref.shape[1]
    x1 = (x_ref[...]
          + jnp.dot(ya_ref[...], wout_ref[0:half, :], preferred_element_type=F32)
          + jnp.dot(ysb_ref[...], wout_ref[half:, :], preferred_element_type=F32))
    h = _rms(x1, fg_ref[...]).astype(BF16)
    gu = jnp.dot(h, wgu_ref[...], preferred_element_type=F32)
    gate = gu[:, 0:FFN_HIDDEN]
    act = (gate * jax.nn.sigmoid(gate) * gu[:, FFN_HIDDEN:]).astype(BF16)
    o_ref[...] = x1 + jnp.dot(act, wd_ref[...], preferred_element_type=F32)


def _const_spec(shape):
    nd = len(shape)
    return pl.BlockSpec(shape, lambda *_: (0,) * nd, pipeline_mode=pl.Buffered(1))


def _layer_spec(stacked, layer):
    nd = stacked.ndim - 1
    return pl.BlockSpec((None,) + stacked.shape[1:], lambda *_: (layer,) + (0,) * nd,
                        pipeline_mode=pl.Buffered(1))


def _mix_in(x2, layer, w_in, consts):
    n = x2.shape[0]
    tm = TM_IN
    row = lambda w: pl.BlockSpec((tm, w), lambda i: (i, 0))
    g, rest = consts[0], consts[1:]
    return pl.pallas_call(
        _mix_in_kernel,
        grid=(n // tm,),
        in_specs=[row(D_MODEL), _const_spec(g.shape), _layer_spec(w_in, layer)]
        + [_const_spec(c.shape) for c in rest],
        out_specs=[row(CONV_WIDTH + SG_WIDTH), row(SB_WIDTH), row(SB_WIDTH), row(SB_WIDTH)],
        out_shape=[jax.ShapeDtypeStruct((n, CONV_WIDTH + SG_WIDTH), BF16)]
        + [jax.ShapeDtypeStruct((n, SB_WIDTH), BF16)] * 3,
        scratch_shapes=[pltpu.VMEM((HALO + tm, CONV_WIDTH), F32)],
        compiler_params=pltpu.CompilerParams(
            dimension_semantics=("arbitrary",), vmem_limit_bytes=VMEM_LIMIT),
        name="mix_in",
    )(x2, g, w_in, *rest)


def _attn(q, k, v, u, og):
    b, s, w = q.shape
    return pl.pallas_call(
        _attn_kernel,
        grid=(b, s // TQ),
        in_specs=[pl.BlockSpec((1, TQ, w), lambda bi, qi: (bi, qi, 0)),
                  pl.BlockSpec((1, s, w), lambda bi, qi: (bi, 0, 0)),
                  pl.BlockSpec((1, s, w), lambda bi, qi: (bi, 0, 0)),
                  _const_spec(u.shape), _const_spec(og.shape)],
        out_specs=pl.BlockSpec((1, TQ, w), lambda bi, qi: (bi, qi, 0)),
        out_shape=jax.ShapeDtypeStruct((b, s, w), BF16),
        scratch_shapes=[pltpu.VMEM((SB_HEADS // 2, 2 * TQ, LANES), F32),
                        pltpu.VMEM((SB_HEADS // 2, 2 * TQ, LANES), F32),
                        pltpu.VMEM((SB_HEADS // 2, 2 * TQ, TK), BF16),
                        pltpu.VMEM((2, SB_HEADS // 2, 2 * TQ, TK), F32)],
        compiler_params=pltpu.CompilerParams(
            dimension_semantics=("parallel", "arbitrary"), vmem_limit_bytes=VMEM_LIMIT),
        name="sb_attn",
    )(q, k, v, u, og)


def _out_ffn(x2, ya, ysb, layer, wout, fg, wgu, wd):
    n = x2.shape[0]
    tm = TM_FFN
    row = lambda w: pl.BlockSpec((tm, w), lambda i: (i, 0))
    return pl.pallas_call(
        _out_ffn_kernel,
        grid=(n // tm,),
        in_specs=[row(D_MODEL), row(ya.shape[1]), row(ysb.shape[1]),
                  _layer_spec(wout, layer), _const_spec(fg.shape), _layer_spec(wgu, layer), _layer_spec(wd, layer)],
        out_specs=row(D_MODEL),
        out_shape=jax.ShapeDtypeStruct((n, D_MODEL), F32),
        compiler_params=pltpu.CompilerParams(
            dimension_semantics=("parallel",), vmem_limit_bytes=VMEM_LIMIT),
        name="out_ffn",
    )(x2, ya, ysb, wout, fg, wgu, wd)


def kernel(x, mix_norm_g, w_in, conv_w, conv_b, conv_ln_g, conv_ln_b, sg_ln_g, sg_ln_b, sg_w, sg_b,
           q_norm_g, k_norm_g, out_norm_g, w_out, ffn_norm_g, w_gate_up, w_down):
    bsz, seq, d = x.shape
    assert (seq, d) == (SEQ, D_MODEL) and seq % TM_IN == 0 and seq % TK == 0
    n = bsz * seq
    x2 = x.reshape(n, d)

    hid = jnp.arange(SB_WIDTH) // HEAD_DIM
    bd = jnp.where(hid[:, None] == hid[None, :], 1.0 / HEAD_DIM, 0.0).astype(BF16)
    ki = jnp.arange(TK)
    u = (ki[:, None] >= ki[None, :]).astype(BF16)
    row2 = lambda a: a.reshape(1, -1)
    w_in, w_out, w_gate_up, w_down = (w.astype(BF16) for w in (w_in, w_out, w_gate_up, w_down))

    for l in range(DEPTH):
        cw = jnp.pad(conv_w[l], ((0, HALO - CONV_KERNEL), (0, 0)))
        sgb = jnp.repeat(sg_b[l].T, HEAD_DIM, axis=1)
        qg = jnp.tile(q_norm_g[l], SB_HEADS).reshape(1, -1) * (HEAD_DIM ** -0.5)
        kg = jnp.tile(k_norm_g[l], SB_HEADS).reshape(1, -1)
        og = row2(out_norm_g[l])
        ya, q, k, v = _mix_in(
            x2, l, w_in,
            [row2(mix_norm_g[l]), cw, row2(conv_b[l]), row2(conv_ln_g[l]), row2(conv_ln_b[l]),
             row2(sg_ln_g[l]), row2(sg_ln_b[l]), sg_w[l], sgb, qg, kg, og[:, :CONV_WIDTH + SG_WIDTH], bd])
        shp = (bsz, seq, SB_WIDTH)
        ysb = _attn(q.reshape(shp), k.reshape(shp), v.reshape(shp), u, og[:, CONV_WIDTH + SG_WIDTH:])
        x2 = _out_ffn(x2, ya, ysb.reshape(n, SB_WIDTH), l, w_out, row2(ffn_norm_g[l]), w_gate_up, w_down)
    return x2.reshape(bsz, seq, d)
```

```python
import functools
import math

import jax
import jax.numpy as jnp
from jax import lax
from jax.experimental import pallas as pl
from jax.experimental.pallas import tpu as pltpu

F32 = jnp.float32
BF16 = jnp.bfloat16

D_MODEL = 1024
SEQ = 2048
DEPTH = 4
HEAD_DIM = 64
CONV_WIDTH = 256
SG_WIDTH = 256
SB_WIDTH = 512
SB_HEADS = SB_WIDTH // HEAD_DIM
SG_HEADS = SG_WIDTH // HEAD_DIM
CONV_KERNEL = 31
SG_CHUNK = 128
OFF_SG = 2 * CONV_WIDTH
OFF_SB = OFF_SG + 2 * SG_WIDTH
IN_WIDTH = OFF_SB + 3 * SB_WIDTH
FFN_HIDDEN = 2816
RMS_EPS = 1e-6
LN_EPS = 1e-5
LOG2E = math.log2(math.e)
SKIP_CARRY = 128.0

LANES = 128
SUBLANES = 8
HALO = 32
TM_IN = 1024
TM_FFN = 512
TQ = 256
TK = 256
VMEM_LIMIT = 56 * 1024 * 1024


def _rms(x, g):
    return x * lax.rsqrt(jnp.mean(x * x, axis=-1, keepdims=True) + RMS_EPS) * g


def _ln(x, g, b):
    mu = jnp.mean(x, axis=-1, keepdims=True)
    xc = x - mu
    var = jnp.mean(xc * xc, axis=-1, keepdims=True)
    return xc * lax.rsqrt(var + LN_EPS) * g + b


def _mix_in_kernel(x_ref, g_ref, w_in_ref, cw_ref, cb_ref, clg_ref, clb_ref,
                   slg_ref, slb_ref, sgw_ref, sgb_ref, qg_ref, kg_ref, og_ref, bd_ref,
                   ya_ref, q_ref, k_ref, v_ref, hist_ref):
    tm = x_ref.shape[0]
    tiles_per_seq = SEQ // tm
    n_chunks = tm // SG_CHUNK

    @pl.when(pl.program_id(0) % tiles_per_seq == 0)
    def _():
        hist_ref[0:HALO, :] = jnp.zeros((HALO, CONV_WIDTH), F32)

    h = _rms(x_ref[...], g_ref[...]).astype(BF16)

    def project(lo, hi):
        return jnp.dot(h, w_in_ref[:, lo:hi], preferred_element_type=F32)

    pa = project(0, OFF_SG)
    hist_ref[HALO:HALO + tm, :] = pa[:, 0:CONV_WIDTH] * jax.nn.sigmoid(pa[:, CONV_WIDTH:])
    pb = project(OFF_SG, OFF_SB)
    first = HALO - (CONV_KERNEL - 1)
    win = SG_CHUNK + HALO
    for c in range(n_chunks):
        r0 = c * SG_CHUNK
        window = hist_ref[r0:r0 + win, :]
        conv = jnp.zeros((SG_CHUNK, CONV_WIDTH), F32) + cb_ref[...]
        for b in range(SUBLANES):
            taps = [j for j in range(CONV_KERNEL) if (first + j) % SUBLANES == b]
            sh = window if b == 0 else pltpu.roll(window, win - b, 0)
            for j in taps:
                a0 = first + j - b
                conv = conv + sh[a0:a0 + SG_CHUNK, :] * cw_ref[j:j + 1, :]
        yc = _ln(conv, clg_ref[...], clb_ref[...])
        yc = yc * jax.nn.sigmoid(yc)
        ya_ref[r0:r0 + SG_CHUNK, 0:CONV_WIDTH] = _rms(yc, og_ref[:, 0:CONV_WIDTH]).astype(BF16)
    hist_ref[0:HALO, :] = hist_ref[tm:tm + HALO, :]

    pc = project(OFF_SB, IN_WIDTH)
    row = lax.broadcasted_iota(jnp.int32, (SG_CHUNK, SG_CHUNK), 0)
    col = lax.broadcasted_iota(jnp.int32, (SG_CHUNK, SG_CHUNK), 1)
    lane = lax.broadcasted_iota(jnp.int32, (SG_CHUNK, SG_WIDTH), 1)
    ws = [jnp.where(row >= col, sgw_ref[hd], 0.0).astype(BF16) for hd in range(SG_HEADS)]
    for c in range(n_chunks):
        rows = slice(c * SG_CHUNK, (c + 1) * SG_CHUNK)
        uv = pb[rows, :]
        uv = 0.5 * uv * (1.0 + lax.erf(uv * (1.0 / math.sqrt(2.0))))
        vc = _ln(uv[:, SG_WIDTH:], slg_ref[...], slb_ref[...]).astype(BF16)
        mixed = jnp.dot(ws[SG_HEADS - 1], vc, preferred_element_type=F32)
        for hd in range(SG_HEADS - 2, -1, -1):
            mixed = jnp.where(lane < (hd + 1) * HEAD_DIM,
                              jnp.dot(ws[hd], vc, preferred_element_type=F32), mixed)
        ysg = uv[:, 0:SG_WIDTH] * (mixed + sgb_ref[...])
        ya_ref[rows, CONV_WIDTH:] = _rms(ysg, og_ref[:, CONV_WIDTH:]).astype(BF16)

    def head_norm(t, g):
        ms = jnp.dot((t * t).astype(BF16), bd_ref[...], preferred_element_type=F32)
        return t * lax.rsqrt(ms + RMS_EPS) * g

    q_ref[...] = head_norm(pc[:, 0:SB_WIDTH], qg_ref[...]).astype(BF16)
    k_ref[...] = head_norm(pc[:, SB_WIDTH:2 * SB_WIDTH], kg_ref[...]).astype(BF16)
    v_ref[...] = pc[:, 2 * SB_WIDTH:].astype(BF16)


def _attn_kernel(q_ref, k_ref, v_ref, u_ref, og_ref, o_ref, acc_ref, carry_ref, att_ref, z_ref,
                 floor_ref, last_ref):
    n_pairs = SB_HEADS // 2
    qi = pl.program_id(1)
    jd = qi // (TK // TQ)
    acc_ref[...] = jnp.zeros_like(acc_ref)
    carry_ref[...] = jnp.zeros_like(carry_ref)

    lane = lax.broadcasted_iota(jnp.int32, (TQ, LANES), 1)
    qs = []
    for p in range(n_pairs):
        qp = q_ref[0, :, p * LANES:(p + 1) * LANES]
        qs.append(jnp.concatenate([jnp.where(lane < HEAD_DIM, qp, jnp.zeros_like(qp)),
                                   jnp.where(lane >= HEAD_DIM, qp, jnp.zeros_like(qp))], axis=0))

    half = n_pairs // 2

    def step(j, slot, masked, pending, prefetch):
        if masked:
            t_pos = qi * TQ + lax.broadcasted_iota(jnp.int32, (2 * TQ, TK), 0) % TQ
            s_pos = j * TK + lax.broadcasted_iota(jnp.int32, (2 * TQ, TK), 1)
            causal = s_pos < t_pos

        def value_dot(p):
            k0 = pl.multiple_of(pending * TK, TK)
            return jnp.dot(att_ref[p], v_ref[0, pl.ds(k0, TK), p * LANES:(p + 1) * LANES],
                           preferred_element_type=F32)

        def score_dot(p):
            k0 = pl.multiple_of(prefetch * TK, TK)
            return lax.dot_general(qs[p], k_ref[0, pl.ds(k0, TK), p * LANES:(p + 1) * LANES],
                                   (((1,), (1,)), ((), ())), preferred_element_type=F32)

        def softplus(p):
            z = z_ref[slot, p]
            sp = jnp.maximum(z, 0.0) + jnp.log(1.0 + jnp.exp2(jnp.abs(z) * -LOG2E))
            if masked:
                sp = jnp.where(causal, sp, 0.0)
            return sp.astype(BF16)

        def later_dot(sps):
            return jnp.dot(jnp.concatenate(sps, axis=0), u_ref[...], preferred_element_type=F32)

        def weights(p, later):
            lat = later[(p % half) * 2 * TQ:(p % half + 1) * 2 * TQ, :]
            carry = carry_ref[p]
            att = jnp.exp(z_ref[slot, p] - lat - jnp.concatenate([carry] * (TK // LANES), axis=1))
            if masked:
                att = jnp.where(causal, att, 0.0)
            att_ref[p] = att.astype(BF16)
            carry = carry + lat[:, 0:1]
            carry_ref[p] = carry
            return jnp.min(carry)

        groups = [list(range(g * half, (g + 1) * half)) for g in range(2)]
        vals = {}
        if pending is not None:
            for p in groups[0]:
                vals[p] = value_dot(p)
        sps0 = [softplus(p) for p in groups[0]]
        later0 = later_dot(sps0)
        if pending is not None:
            for p in groups[1]:
                vals[p] = value_dot(p)
            for p in groups[0]:
                acc_ref[p] += vals[p]
        sps1 = [softplus(p) for p in groups[1]]
        later1 = later_dot(sps1)
        nxt = {}
        if prefetch is not None:
            for p in groups[0]:
                nxt[p] = score_dot(p)
        if pending is not None:
            for p in groups[1]:
                acc_ref[p] += vals[p]
        floors = [weights(p, later0) for p in groups[0]]
        if prefetch is not None:
            for p in groups[0]:
                z_ref[1 - slot, p] = nxt[p]
            for p in groups[1]:
                nxt[p] = score_dot(p)
        floors += [weights(p, later1) for p in groups[1]]
        if prefetch is not None:
            for p in groups[1]:
                z_ref[1 - slot, p] = nxt[p]
        floor_ref[0] = functools.reduce(jnp.minimum, floors)

    k0 = pl.multiple_of(jd * TK, TK)
    for p in range(n_pairs):
        z_ref[0, p] = lax.dot_general(qs[p], k_ref[0, pl.ds(k0, TK), p * LANES:(p + 1) * LANES],
                                      (((1,), (1,)), ((), ())), preferred_element_type=F32)
    step(jd, 0, True, None, jnp.maximum(jd - 1, 0))
    last_ref[0] = jd

    def two_steps(state):
        j, _ = state
        step(j, 1, False, j + 1, jnp.maximum(j - 1, 0))
        go_on = jnp.logical_and(j >= 1, floor_ref[0] <= SKIP_CARRY)

        @pl.when(go_on)
        def _():
            step(j - 1, 0, False, j, jnp.maximum(j - 2, 0))

        last_ref[0] = jnp.where(go_on, j - 1, j)
        return j - 2, floor_ref[0]

    lax.while_loop(lambda state: jnp.logical_and(state[0] >= 0, state[1] <= SKIP_CARRY),
                   two_steps, (jd - 1, floor_ref[0]))

    k0 = pl.multiple_of(last_ref[0] * TK, TK)
    for p in range(n_pairs):
        acc_ref[p] += jnp.dot(att_ref[p], v_ref[0, pl.ds(k0, TK), p * LANES:(p + 1) * LANES],
                              preferred_element_type=F32)

    y = jnp.concatenate(
        [jnp.where(lane < HEAD_DIM, acc_ref[p, 0:TQ, :], acc_ref[p, TQ:, :]) for p in range(n_pairs)], axis=1)
    o_ref[0] = _rms(y, og_ref[...]).astype(BF16)


def _out_ffn_kernel(x_ref, ya_ref, ysb_ref, wout_ref, fg_ref, wgu_ref, wd_ref, o_ref):
    half = ya_ref.shape[1]
    x1 = (x_ref[...]
          + jnp.dot(ya_ref[...], wout_ref[0:half, :], preferred_element_type=F32)
          + jnp.dot(ysb_ref[...], wout_ref[half:, :], preferred_element_type=F32))
    h = _rms(x1, fg_ref[...]).astype(BF16)
    gu = jnp.dot(h, wgu_ref[...], preferred_element_type=F32)
    gate = gu[:, 0:FFN_HIDDEN]
    act = (gate * jax.nn.sigmoid(gate) * gu[:, FFN_HIDDEN:]).astype(BF16)
    o_ref[...] = x1 + jnp.dot(act, wd_ref[...], preferred_element_type=F32)


def _const_spec(shape):
    nd = len(shape)
    return pl.BlockSpec(shape, lambda *_: (0,) * nd, pipeline_mode=pl.Buffered(1))


def _layer_spec(stacked, layer):
    nd = stacked.ndim - 1
    return pl.BlockSpec((None,) + stacked.shape[1:], lambda *_: (layer,) + (0,) * nd,
                        pipeline_mode=pl.Buffered(1))


def _mix_in(x2, layer, w_in, consts):
    n = x2.shape[0]
    tm = TM_IN
    row = lambda w: pl.BlockSpec((tm, w), lambda i: (i, 0))
    g, rest = consts[0], consts[1:]
    return pl.pallas_call(
        _mix_in_kernel,
        grid=(n // tm,),
        in_specs=[row(D_MODEL), _const_spec(g.shape), _layer_spec(w_in, layer)]
        + [_const_spec(c.shape) for c in rest],
        out_specs=[row(CONV_WIDTH + SG_WIDTH), row(SB_WIDTH), row(SB_WIDTH), row(SB_WIDTH)],
        out_shape=[jax.ShapeDtypeStruct((n, CONV_WIDTH + SG_WIDTH), BF16)]
        + [jax.ShapeDtypeStruct((n, SB_WIDTH), BF16)] * 3,
        scratch_shapes=[pltpu.VMEM((HALO + tm, CONV_WIDTH), F32)],
        compiler_params=pltpu.CompilerParams(
            dimension_semantics=("arbitrary",), vmem_limit_bytes=VMEM_LIMIT),
        name="mix_in",
    )(x2, g, w_in, *rest)


def _attn(q, k, v, u, og):
    b, s, w = q.shape
    return pl.pallas_call(
        _attn_kernel,
        grid=(b, s // TQ),
        in_specs=[pl.BlockSpec((1, TQ, w), lambda bi, qi: (bi, qi, 0)),
                  pl.BlockSpec((1, s, w), lambda bi, qi: (bi, 0, 0)),
                  pl.BlockSpec((1, s, w), lambda bi, qi: (bi, 0, 0)),
                  _const_spec(u.shape), _const_spec(og.shape)],
        out_specs=pl.BlockSpec((1, TQ, w), lambda bi, qi: (bi, qi, 0)),
        out_shape=jax.ShapeDtypeStruct((b, s, w), BF16),
        scratch_shapes=[pltpu.VMEM((SB_HEADS // 2, 2 * TQ, LANES), F32),
                        pltpu.VMEM((SB_HEADS // 2, 2 * TQ, LANES), F32),
                        pltpu.VMEM((SB_HEADS // 2, 2 * TQ, TK), BF16),
                        pltpu.VMEM((2, SB_HEADS // 2, 2 * TQ, TK), F32),
                        pltpu.SMEM((1,), F32), pltpu.SMEM((1,), jnp.int32)],
        compiler_params=pltpu.CompilerParams(
            dimension_semantics=("parallel", "arbitrary"), vmem_limit_bytes=VMEM_LIMIT),
        name="sb_attn",
    )(q, k, v, u, og)


def _out_ffn(x2, ya, ysb, layer, wout, fg, wgu, wd):
    n = x2.shape[0]
    tm = TM_FFN
    row = lambda w: pl.BlockSpec((tm, w), lambda i: (i, 0))
    return pl.pallas_call(
        _out_ffn_kernel,
        grid=(n // tm,),
        in_specs=[row(D_MODEL), row(ya.shape[1]), row(ysb.shape[1]),
                  _layer_spec(wout, layer), _const_spec(fg.shape), _layer_spec(wgu, layer), _layer_spec(wd, layer)],
        out_specs=row(D_MODEL),
        out_shape=jax.ShapeDtypeStruct((n, D_MODEL), F32),
        compiler_params=pltpu.CompilerParams(
            dimension_semantics=("parallel",), vmem_limit_bytes=VMEM_LIMIT),
        name="out_ffn",
    )(x2, ya, ysb, wout, fg, wgu, wd)


def kernel(x, mix_norm_g, w_in, conv_w, conv_b, conv_ln_g, conv_ln_b, sg_ln_g, sg_ln_b, sg_w, sg_b,
           q_norm_g, k_norm_g, out_norm_g, w_out, ffn_norm_g, w_gate_up, w_down):
    bsz, seq, d = x.shape
    assert (seq, d) == (SEQ, D_MODEL) and seq % TM_IN == 0 and seq % TK == 0
    n = bsz * seq
    x2 = x.reshape(n, d)

    hid = jnp.arange(SB_WIDTH) // HEAD_DIM
    bd = jnp.where(hid[:, None] == hid[None, :], 1.0 / HEAD_DIM, 0.0).astype(BF16)
    ki = jnp.arange(TK)
    u = (ki[:, None] >= ki[None, :]).astype(BF16)
    row2 = lambda a: a.reshape(1, -1)
    w_in, w_out, w_gate_up, w_down = (w.astype(BF16) for w in (w_in, w_out, w_gate_up, w_down))

    for l in range(DEPTH):
        cw = jnp.pad(conv_w[l], ((0, HALO - CONV_KERNEL), (0, 0)))
        sgb = jnp.repeat(sg_b[l].T, HEAD_DIM, axis=1)
        qg = jnp.tile(q_norm_g[l], SB_HEADS).reshape(1, -1) * (HEAD_DIM ** -0.5)
        kg = jnp.tile(k_norm_g[l], SB_HEADS).reshape(1, -1)
        og = row2(out_norm_g[l])
        ya, q, k, v = _mix_in(
            x2, l, w_in,
            [row2(mix_norm_g[l]), cw, row2(conv_b[l]), row2(conv_ln_g[l]), row2(conv_ln_b[l]),
             row2(sg_ln_g[l]), row2(sg_ln_b[l]), sg_w[l], sgb, qg, kg, og[:, :CONV_WIDTH + SG_WIDTH], bd])
        shp = (bsz, seq, SB_WIDTH)
        ysb = _attn(q.reshape(shp), k.reshape(shp), v.reshape(shp), u, og[:, CONV_WIDTH + SG_WIDTH:])
        x2 = _out_ffn(x2, ya, ysb.reshape(n, SB_WIDTH), l, w_out, row2(ffn_norm_g[l]), w_gate_up, w_down)
    return x2.reshape(bsz, seq, d)
```

```python
import functools
import math

import jax
import jax.numpy as jnp
from jax import lax
from jax.experimental import pallas as pl
from jax.experimental.pallas import tpu as pltpu

F32 = jnp.float32
BF16 = jnp.bfloat16

D_MODEL = 1024
SEQ = 2048
DEPTH = 4
HEAD_DIM = 64
CONV_WIDTH = 256
SG_WIDTH = 256
SB_WIDTH = 512
SB_HEADS = SB_WIDTH // HEAD_DIM
SG_HEADS = SG_WIDTH // HEAD_DIM
CONV_KERNEL = 31
SG_CHUNK = 128
OFF_SG = 2 * CONV_WIDTH
OFF_SB = OFF_SG + 2 * SG_WIDTH
IN_WIDTH = OFF_SB + 3 * SB_WIDTH
FFN_HIDDEN = 2816
RMS_EPS = 1e-6
LN_EPS = 1e-5
LOG2E = math.log2(math.e)
SKIP_CARRY = 128.0

LANES = 128
SUBLANES = 8
MXU_WIDTH = 256
HALO = 32
TM_IN = 1024
TM_FFN = 512
TQ = 256
TK = 256
VMEM_LIMIT = 56 * 1024 * 1024


def _rms(x, g):
    return x * lax.rsqrt(jnp.mean(x * x, axis=-1, keepdims=True) + RMS_EPS) * g


def _ln(x, g, b):
    mu = jnp.mean(x, axis=-1, keepdims=True)
    xc = x - mu
    var = jnp.mean(xc * xc, axis=-1, keepdims=True)
    return xc * lax.rsqrt(var + LN_EPS) * g + b


def _mix_in_kernel(x_ref, g_ref, w_in_ref, cw_ref, cb_ref, clg_ref, clb_ref,
                   slg_ref, slb_ref, sgw_ref, sgb_ref, qg_ref, kg_ref, og_ref, bd_ref,
                   ya_ref, q_ref, k_ref, v_ref, hist_ref):
    tm = x_ref.shape[0]
    tiles_per_seq = SEQ // tm
    n_chunks = tm // SG_CHUNK

    @pl.when(pl.program_id(0) % tiles_per_seq == 0)
    def _():
        hist_ref[0:HALO, :] = jnp.zeros((HALO, CONV_WIDTH), F32)

    h = _rms(x_ref[...], g_ref[...]).astype(BF16)

    def project(lo, hi):
        return jnp.dot(h, w_in_ref[:, lo:hi], preferred_element_type=F32)

    pa = project(0, OFF_SG)
    hist_ref[HALO:HALO + tm, :] = pa[:, 0:CONV_WIDTH] * jax.nn.sigmoid(pa[:, CONV_WIDTH:])
    pb = project(OFF_SG, OFF_SB)
    first = HALO - (CONV_KERNEL - 1)
    win = SG_CHUNK + HALO
    for c in range(n_chunks):
        r0 = c * SG_CHUNK
        window = hist_ref[r0:r0 + win, :]
        conv = jnp.zeros((SG_CHUNK, CONV_WIDTH), F32) + cb_ref[...]
        for b in range(SUBLANES):
            taps = [j for j in range(CONV_KERNEL) if (first + j) % SUBLANES == b]
            sh = window if b == 0 else pltpu.roll(window, win - b, 0)
            for j in taps:
                a0 = first + j - b
                conv = conv + sh[a0:a0 + SG_CHUNK, :] * cw_ref[j:j + 1, :]
        yc = _ln(conv, clg_ref[...], clb_ref[...])
        yc = yc * jax.nn.sigmoid(yc)
        ya_ref[r0:r0 + SG_CHUNK, 0:CONV_WIDTH] = _rms(yc, og_ref[:, 0:CONV_WIDTH]).astype(BF16)
    hist_ref[0:HALO, :] = hist_ref[tm:tm + HALO, :]

    pc = project(OFF_SB, IN_WIDTH)
    row = lax.broadcasted_iota(jnp.int32, (SG_CHUNK, SG_CHUNK), 0)
    col = lax.broadcasted_iota(jnp.int32, (SG_CHUNK, SG_CHUNK), 1)
    lane = lax.broadcasted_iota(jnp.int32, (SG_CHUNK, SG_WIDTH), 1)
    ws = [jnp.where(row >= col, sgw_ref[hd], 0.0).astype(BF16) for hd in range(SG_HEADS)]
    for c in range(n_chunks):
        rows = slice(c * SG_CHUNK, (c + 1) * SG_CHUNK)
        uv = pb[rows, :]
        uv = 0.5 * uv * (1.0 + lax.erf(uv * (1.0 / math.sqrt(2.0))))
        vc = _ln(uv[:, SG_WIDTH:], slg_ref[...], slb_ref[...]).astype(BF16)
        mixed = jnp.dot(ws[SG_HEADS - 1], vc, preferred_element_type=F32)
        for hd in range(SG_HEADS - 2, -1, -1):
            mixed = jnp.where(lane < (hd + 1) * HEAD_DIM,
                              jnp.dot(ws[hd], vc, preferred_element_type=F32), mixed)
        ysg = uv[:, 0:SG_WIDTH] * (mixed + sgb_ref[...])
        ya_ref[rows, CONV_WIDTH:] = _rms(ysg, og_ref[:, CONV_WIDTH:]).astype(BF16)

    def head_norm(t, g):
        sq = (t * t).astype(BF16)
        w = bd_ref.shape[0]
        ms = jnp.concatenate([jnp.dot(sq[:, c:c + w], bd_ref[...], preferred_element_type=F32)
                              for c in range(0, SB_WIDTH, w)], axis=1)
        return t * lax.rsqrt(ms + RMS_EPS) * g

    q_ref[...] = head_norm(pc[:, 0:SB_WIDTH], qg_ref[...]).astype(BF16)
    k_ref[...] = head_norm(pc[:, SB_WIDTH:2 * SB_WIDTH], kg_ref[...]).astype(BF16)
    v_ref[...] = pc[:, 2 * SB_WIDTH:].astype(BF16)


def _attn_kernel(q_ref, k_ref, v_ref, u_ref, og_ref, o_ref, acc_ref, carry_ref, z_ref, floor_ref):
    n_pairs = SB_HEADS // 2
    half = n_pairs // 2
    qi = pl.program_id(1)
    jd = qi // (TK // TQ)
    lane = lax.broadcasted_iota(jnp.int32, (TQ, LANES), 1)

    def start():
        acc_ref[...] = jnp.zeros_like(acc_ref)
        carry_ref[...] = jnp.zeros_like(carry_ref)
        qs = []
        for p in range(n_pairs):
            qp = q_ref[0, :, p * LANES:(p + 1) * LANES]
            qs.append(jnp.concatenate([jnp.where(lane < HEAD_DIM, qp, jnp.zeros_like(qp)),
                                       jnp.where(lane >= HEAD_DIM, qp, jnp.zeros_like(qp))], axis=0))
        return qs

    def score_dot(qs, p, j):
        k0 = pl.multiple_of(j * TK, TK)
        return lax.dot_general(qs[p], k_ref[0, pl.ds(k0, TK), p * LANES:(p + 1) * LANES],
                               (((1,), (1,)), ((), ())), preferred_element_type=F32)

    def scores(qs, j, slot):
        for p in range(n_pairs):
            z_ref[slot, p] = score_dot(qs, p, j)

    def step(qs, j, slot, masked, prefetch):
        k0 = pl.multiple_of(j * TK, TK)
        if masked:
            t_pos = qi * TQ + lax.broadcasted_iota(jnp.int32, (2 * TQ, TK), 0) % TQ
            s_pos = j * TK + lax.broadcasted_iota(jnp.int32, (2 * TQ, TK), 1)
            causal = s_pos < t_pos

        def softplus(p):
            z = z_ref[slot, p]
            sp = jnp.maximum(z, 0.0) + jnp.log(1.0 + jnp.exp2(jnp.abs(z) * -LOG2E))
            if masked:
                sp = jnp.where(causal, sp, 0.0)
            return sp.astype(BF16)

        def later_dot(sps):
            return jnp.dot(jnp.concatenate(sps, axis=0), u_ref[...], preferred_element_type=F32)

        def weights(p, later):
            lat = later[(p % half) * 2 * TQ:(p % half + 1) * 2 * TQ, :]
            carry = carry_ref[p]
            att = jnp.exp(z_ref[slot, p] - lat - jnp.concatenate([carry] * (TK // LANES), axis=1))
            if masked:
                att = jnp.where(causal, att, 0.0)
            carry = carry + lat[:, 0:1]
            carry_ref[p] = carry
            return att.astype(BF16), jnp.min(carry)

        def value_dot(p, att):
            return jnp.dot(att, v_ref[0, pl.ds(k0, TK), p * LANES:(p + 1) * LANES], preferred_element_type=F32)

        groups = [list(range(g * half, (g + 1) * half)) for g in range(2)]
        later0 = later_dot([softplus(p) for p in groups[0]])
        later1 = later_dot([softplus(p) for p in groups[1]])
        nxt, vals, floors = {}, {}, []
        if prefetch is not None:
            for p in groups[0]:
                nxt[p] = score_dot(qs, p, prefetch)
        for p in groups[0]:
            att, floor = weights(p, later0)
            floors.append(floor)
            vals[p] = value_dot(p, att)
        if prefetch is not None:
            for p in groups[0]:
                z_ref[1 - slot, p] = nxt[p]
            for p in groups[1]:
                nxt[p] = score_dot(qs, p, prefetch)
        for p in groups[1]:
            att, floor = weights(p, later1)
            floors.append(floor)
            vals[p] = value_dot(p, att)
        for p in groups[0]:
            acc_ref[p] += vals[p]
        if prefetch is not None:
            for p in groups[1]:
                z_ref[1 - slot, p] = nxt[p]
        for p in groups[1]:
            acc_ref[p] += vals[p]
        floor_ref[0] = functools.reduce(jnp.minimum, floors)

    def finish():
        y = jnp.concatenate(
            [jnp.where(lane < HEAD_DIM, acc_ref[p, 0:TQ, :], acc_ref[p, TQ:, :]) for p in range(n_pairs)], axis=1)
        o_ref[0] = _rms(y, og_ref[...]).astype(BF16)

    @pl.when(jd == 0)
    def _():
        qs = start()
        scores(qs, 0, 0)
        step(qs, 0, 0, True, None)
        finish()

    @pl.when(jd > 0)
    def _():
        qs = start()
        scores(qs, jd, 0)
        step(qs, jd, 0, True, jd - 1)
        step(qs, jd - 1, 1, False, None)
        finish()

        @pl.when(jnp.logical_and(jd >= 2, floor_ref[0] <= SKIP_CARRY))
        def _():
            scores(qs, jd - 2, 1)

            def two_steps(state):
                j, _ = state
                step(qs, j, 1, False, jnp.maximum(j - 1, 0))

                @pl.when(jnp.logical_and(j >= 1, floor_ref[0] <= SKIP_CARRY))
                def _():
                    step(qs, j - 1, 0, False, jnp.maximum(j - 2, 0))

                return j - 2, floor_ref[0]

            lax.while_loop(lambda state: jnp.logical_and(state[0] >= 0, state[1] <= SKIP_CARRY),
                           two_steps, (jd - 2, floor_ref[0]))
            finish()


def _out_ffn_kernel(x_ref, ya_ref, ysb_ref, wout_ref, fg_ref, wgu_ref, wd_ref, o_ref):
    half = ya_ref.shape[1]
    x1 = (x_ref[...]
          + jnp.dot(ya_ref[...], wout_ref[0:half, :], preferred_element_type=F32)
          + jnp.dot(ysb_ref[...], wout_ref[half:, :], preferred_element_type=F32))
    h = _rms(x1, fg_ref[...]).astype(BF16)
    gu = jnp.dot(h, wgu_ref[...], preferred_element_type=F32)
    gate = gu[:, 0:FFN_HIDDEN]
    act = (gate * jax.nn.sigmoid(gate) * gu[:, FFN_HIDDEN:]).astype(BF16)
    o_ref[...] = x1 + jnp.dot(act, wd_ref[...], preferred_element_type=F32)


def _const_spec(shape):
    nd = len(shape)
    return pl.BlockSpec(shape, lambda *_: (0,) * nd, pipeline_mode=pl.Buffered(1))


def _layer_spec(stacked, layer):
    nd = stacked.ndim - 1
    return pl.BlockSpec((None,) + stacked.shape[1:], lambda *_: (layer,) + (0,) * nd,
                        pipeline_mode=pl.Buffered(1))


def _mix_in(x2, layer, w_in, consts):
    n = x2.shape[0]
    tm = TM_IN
    row = lambda w: pl.BlockSpec((tm, w), lambda i: (i, 0))
    g, rest = consts[0], consts[1:]
    return pl.pallas_call(
        _mix_in_kernel,
        grid=(n // tm,),
        in_specs=[row(D_MODEL), _const_spec(g.shape), _layer_spec(w_in, layer)]
        + [_const_spec(c.shape) for c in rest],
        out_specs=[row(CONV_WIDTH + SG_WIDTH), row(SB_WIDTH), row(SB_WIDTH), row(SB_WIDTH)],
        out_shape=[jax.ShapeDtypeStruct((n, CONV_WIDTH + SG_WIDTH), BF16)]
        + [jax.ShapeDtypeStruct((n, SB_WIDTH), BF16)] * 3,
        scratch_shapes=[pltpu.VMEM((HALO + tm, CONV_WIDTH), F32)],
        compiler_params=pltpu.CompilerParams(
            dimension_semantics=("arbitrary",), vmem_limit_bytes=VMEM_LIMIT),
        name="mix_in",
    )(x2, g, w_in, *rest)


def _attn(q, k, v, u, og):
    b, s, w = q.shape
    return pl.pallas_call(
        _attn_kernel,
        grid=(b, s // TQ),
        in_specs=[pl.BlockSpec((1, TQ, w), lambda bi, qi: (bi, qi, 0)),
                  pl.BlockSpec((1, s, w), lambda bi, qi: (bi, 0, 0)),
                  pl.BlockSpec((1, s, w), lambda bi, qi: (bi, 0, 0)),
                  _const_spec(u.shape), _const_spec(og.shape)],
        out_specs=pl.BlockSpec((1, TQ, w), lambda bi, qi: (bi, qi, 0)),
        out_shape=jax.ShapeDtypeStruct((b, s, w), BF16),
        scratch_shapes=[pltpu.VMEM((SB_HEADS // 2, 2 * TQ, LANES), F32),
                        pltpu.VMEM((SB_HEADS // 2, 2 * TQ, LANES), F32),
                        pltpu.VMEM((2, SB_HEADS // 2, 2 * TQ, TK), F32),
                        pltpu.SMEM((1,), F32)],
        compiler_params=pltpu.CompilerParams(
            dimension_semantics=("parallel", "arbitrary"), vmem_limit_bytes=VMEM_LIMIT),
        name="sb_attn",
    )(q, k, v, u, og)


def _out_ffn(x2, ya, ysb, layer, wout, fg, wgu, wd):
    n = x2.shape[0]
    tm = TM_FFN
    row = lambda w: pl.BlockSpec((tm, w), lambda i: (i, 0))
    return pl.pallas_call(
        _out_ffn_kernel,
        grid=(n // tm,),
        in_specs=[row(D_MODEL), row(ya.shape[1]), row(ysb.shape[1]),
                  _layer_spec(wout, layer), _const_spec(fg.shape), _layer_spec(wgu, layer), _layer_spec(wd, layer)],
        out_specs=row(D_MODEL),
        out_shape=jax.ShapeDtypeStruct((n, D_MODEL), F32),
        compiler_params=pltpu.CompilerParams(
            dimension_semantics=("parallel",), vmem_limit_bytes=VMEM_LIMIT),
        name="out_ffn",
    )(x2, ya, ysb, wout, fg, wgu, wd)


def kernel(x, mix_norm_g, w_in, conv_w, conv_b, conv_ln_g, conv_ln_b, sg_ln_g, sg_ln_b, sg_w, sg_b,
           q_norm_g, k_norm_g, out_norm_g, w_out, ffn_norm_g, w_gate_up, w_down):
    bsz, seq, d = x.shape
    assert (seq, d) == (SEQ, D_MODEL) and seq % TM_IN == 0 and seq % TK == 0
    n = bsz * seq
    x2 = x.reshape(n, d)

    hid = jnp.arange(MXU_WIDTH) // HEAD_DIM
    bd = jnp.where(hid[:, None] == hid[None, :], 1.0 / HEAD_DIM, 0.0).astype(BF16)
    ki = jnp.arange(TK)
    u = (ki[:, None] >= ki[None, :]).astype(BF16)
    row2 = lambda a: a.reshape(1, -1)
    w_in, w_out, w_gate_up, w_down = (w.astype(BF16) for w in (w_in, w_out, w_gate_up, w_down))

    for l in range(DEPTH):
        cw = jnp.pad(conv_w[l], ((0, HALO - CONV_KERNEL), (0, 0)))
        sgb = jnp.repeat(sg_b[l].T, HEAD_DIM, axis=1)
        qg = jnp.tile(q_norm_g[l], SB_HEADS).reshape(1, -1) * (HEAD_DIM ** -0.5)
        kg = jnp.tile(k_norm_g[l], SB_HEADS).reshape(1, -1)
        og = row2(out_norm_g[l])
        ya, q, k, v = _mix_in(
            x2, l, w_in,
            [row2(mix_norm_g[l]), cw, row2(conv_b[l]), row2(conv_ln_g[l]), row2(conv_ln_b[l]),
             row2(sg_ln_g[l]), row2(sg_ln_b[l]), sg_w[l], sgb, qg, kg, og[:, :CONV_WIDTH + SG_WIDTH], bd])
        shp = (bsz, seq, SB_WIDTH)
        ysb = _attn(q.reshape(shp), k.reshape(shp), v.reshape(shp), u, og[:, CONV_WIDTH + SG_WIDTH:])
        x2 = _out_ffn(x2, ya, ysb.reshape(n, SB_WIDTH), l, w_out, row2(ffn_norm_g[l]), w_gate_up, w_down)
    return x2.reshape(bsz, seq, d)
```

```python
import functools
import math

import jax
import jax.numpy as jnp
from jax import lax
from jax.experimental import pallas as pl
from jax.experimental.pallas import tpu as pltpu

F32 = jnp.float32
BF16 = jnp.bfloat16

D_MODEL = 1024
SEQ = 2048
DEPTH = 4
HEAD_DIM = 64
CONV_WIDTH = 256
SG_WIDTH = 256
SB_WIDTH = 512
SB_HEADS = SB_WIDTH // HEAD_DIM
SG_HEADS = SG_WIDTH // HEAD_DIM
CONV_KERNEL = 31
SG_CHUNK = 128
OFF_SG = 2 * CONV_WIDTH
OFF_SB = OFF_SG + 2 * SG_WIDTH
IN_WIDTH = OFF_SB + 3 * SB_WIDTH
FFN_HIDDEN = 2816
RMS_EPS = 1e-6
LN_EPS = 1e-5
LOG2E = math.log2(math.e)
SKIP_CARRY = 128.0

LANES = 128
SUBLANES = 8
BF16_SUBLANES = 16
MXU_WIDTH = 256
HALO = 32
TM_IN = 1024
TM_FFN = 512
TQ = 256
TK = 256
VMEM_LIMIT = 56 * 1024 * 1024


def _rms(x, g):
    return x * lax.rsqrt(jnp.mean(x * x, axis=-1, keepdims=True) + RMS_EPS) * g


def _ln(x, g, b):
    mu = jnp.mean(x, axis=-1, keepdims=True)
    xc = x - mu
    var = jnp.mean(xc * xc, axis=-1, keepdims=True)
    return xc * lax.rsqrt(var + LN_EPS) * g + b


def _mix_in_kernel(x_ref, g_ref, w_in_ref, cw_ref, cb_ref, clg_ref, clb_ref,
                   slg_ref, slb_ref, sgw_ref, sgb_ref, qg_ref, kg_ref, og_ref, bd_ref,
                   ya_ref, q_ref, k_ref, v_ref, hist_ref):
    tm = x_ref.shape[0]
    tiles_per_seq = SEQ // tm
    n_chunks = tm // SG_CHUNK

    @pl.when(pl.program_id(0) % tiles_per_seq == 0)
    def _():
        hist_ref[0:HALO, :] = jnp.zeros((HALO, CONV_WIDTH), F32)

    h = _rms(x_ref[...], g_ref[...]).astype(BF16)

    def project(lo, hi):
        return jnp.dot(h, w_in_ref[:, lo:hi], preferred_element_type=F32)

    pa = project(0, OFF_SG)
    hist_ref[HALO:HALO + tm, :] = pa[:, 0:CONV_WIDTH] * jax.nn.sigmoid(pa[:, CONV_WIDTH:])
    pb = project(OFF_SG, OFF_SB)
    first = HALO - (CONV_KERNEL - 1)
    win = SG_CHUNK + HALO
    for c in range(n_chunks):
        r0 = c * SG_CHUNK
        window = hist_ref[r0:r0 + win, :]
        conv = jnp.zeros((SG_CHUNK, CONV_WIDTH), F32) + cb_ref[...]
        for b in range(SUBLANES):
            taps = [j for j in range(CONV_KERNEL) if (first + j) % SUBLANES == b]
            sh = window if b == 0 else pltpu.roll(window, win - b, 0)
            for j in taps:
                a0 = first + j - b
                conv = conv + sh[a0:a0 + SG_CHUNK, :] * cw_ref[j:j + 1, :]
        yc = _ln(conv, clg_ref[...], clb_ref[...])
        yc = yc * jax.nn.sigmoid(yc)
        ya_ref[r0:r0 + SG_CHUNK, 0:CONV_WIDTH] = _rms(yc, og_ref[:, 0:CONV_WIDTH]).astype(BF16)
    hist_ref[0:HALO, :] = hist_ref[tm:tm + HALO, :]

    pc = project(OFF_SB, IN_WIDTH)
    row = lax.broadcasted_iota(jnp.int32, (SG_CHUNK, SG_CHUNK), 0)
    col = lax.broadcasted_iota(jnp.int32, (SG_CHUNK, SG_CHUNK), 1)
    lane = lax.broadcasted_iota(jnp.int32, (SG_CHUNK, SG_WIDTH), 1)
    ws = [jnp.where(row >= col, sgw_ref[hd], 0.0).astype(BF16) for hd in range(SG_HEADS)]
    for c in range(n_chunks):
        rows = slice(c * SG_CHUNK, (c + 1) * SG_CHUNK)
        uv = pb[rows, :]
        uv = 0.5 * uv * (1.0 + lax.erf(uv * (1.0 / math.sqrt(2.0))))
        vc = _ln(uv[:, SG_WIDTH:], slg_ref[...], slb_ref[...]).astype(BF16)
        mixed = jnp.dot(ws[SG_HEADS - 1], vc, preferred_element_type=F32)
        for hd in range(SG_HEADS - 2, -1, -1):
            mixed = jnp.where(lane < (hd + 1) * HEAD_DIM,
                              jnp.dot(ws[hd], vc, preferred_element_type=F32), mixed)
        ysg = uv[:, 0:SG_WIDTH] * (mixed + sgb_ref[...])
        ya_ref[rows, CONV_WIDTH:] = _rms(ysg, og_ref[:, CONV_WIDTH:]).astype(BF16)

    def head_norm(t, g):
        sq = (t * t).astype(BF16)
        w = bd_ref.shape[0]
        ms = jnp.concatenate([jnp.dot(sq[:, c:c + w], bd_ref[...], preferred_element_type=F32)
                              for c in range(0, SB_WIDTH, w)], axis=1)
        return t * lax.rsqrt(ms + RMS_EPS) * g

    q_ref[...] = head_norm(pc[:, 0:SB_WIDTH], qg_ref[...]).astype(BF16)
    k_ref[...] = head_norm(pc[:, SB_WIDTH:2 * SB_WIDTH], kg_ref[...]).astype(BF16)
    v_ref[...] = pc[:, 2 * SB_WIDTH:].astype(BF16)


def _attn_kernel(q_ref, k_ref, v_ref, u_ref, og_ref, *refs, n_cast):
    w_refs, o_ref, wb_refs = refs[:n_cast], refs[n_cast], refs[n_cast + 1:2 * n_cast + 1]
    acc_ref, carry_ref, z_ref, floor_ref = refs[2 * n_cast + 1:]
    for w_ref, wb_ref in zip(w_refs, wb_refs):
        wb_ref[...] = w_ref[...].astype(BF16)

    n_pairs = SB_HEADS // 2
    half = n_pairs // 2
    qi = pl.program_id(1)
    jd = qi // (TK // TQ)
    lane = lax.broadcasted_iota(jnp.int32, (TQ, LANES), 1)

    def start():
        acc_ref[...] = jnp.zeros_like(acc_ref)
        carry_ref[...] = jnp.zeros_like(carry_ref)
        qs = []
        for p in range(n_pairs):
            qp = q_ref[0, :, p * LANES:(p + 1) * LANES]
            qs.append(jnp.concatenate([jnp.where(lane < HEAD_DIM, qp, jnp.zeros_like(qp)),
                                       jnp.where(lane >= HEAD_DIM, qp, jnp.zeros_like(qp))], axis=0))
        return qs

    def score_dot(qs, p, j):
        k0 = pl.multiple_of(j * TK, TK)
        return lax.dot_general(qs[p], k_ref[0, pl.ds(k0, TK), p * LANES:(p + 1) * LANES],
                               (((1,), (1,)), ((), ())), preferred_element_type=F32)

    def scores(qs, j, slot):
        for p in range(n_pairs):
            z_ref[slot, p] = score_dot(qs, p, j)

    def step(qs, j, slot, masked, prefetch):
        k0 = pl.multiple_of(j * TK, TK)
        if masked:
            t_pos = qi * TQ + lax.broadcasted_iota(jnp.int32, (2 * TQ, TK), 0) % TQ
            s_pos = j * TK + lax.broadcasted_iota(jnp.int32, (2 * TQ, TK), 1)
            causal = s_pos < t_pos

        def softplus(p):
            z = z_ref[slot, p]
            sp = jnp.maximum(z, 0.0) + jnp.log(1.0 + jnp.exp2(jnp.abs(z) * -LOG2E))
            if masked:
                sp = jnp.where(causal, sp, 0.0)
            return sp.astype(BF16)

        def later_dot(sps):
            return jnp.dot(jnp.concatenate(sps, axis=0), u_ref[...], preferred_element_type=F32)

        def weights(p, later):
            lat = later[(p % half) * 2 * TQ:(p % half + 1) * 2 * TQ, :]
            carry = carry_ref[p]
            att = jnp.exp(z_ref[slot, p] - lat - jnp.concatenate([carry] * (TK // LANES), axis=1))
            if masked:
                att = jnp.where(causal, att, 0.0)
            carry = carry + lat[:, 0:1]
            carry_ref[p] = carry
            return att.astype(BF16), jnp.min(carry)

        def value_dot(p, att):
            return jnp.dot(att, v_ref[0, pl.ds(k0, TK), p * LANES:(p + 1) * LANES], preferred_element_type=F32)

        groups = [list(range(g * half, (g + 1) * half)) for g in range(2)]
        later0 = later_dot([softplus(p) for p in groups[0]])
        later1 = later_dot([softplus(p) for p in groups[1]])
        nxt, vals, floors = {}, {}, []
        if prefetch is not None:
            for p in groups[0]:
                nxt[p] = score_dot(qs, p, prefetch)
        for p in groups[0]:
            att, floor = weights(p, later0)
            floors.append(floor)
            vals[p] = value_dot(p, att)
        if prefetch is not None:
            for p in groups[0]:
                z_ref[1 - slot, p] = nxt[p]
            for p in groups[1]:
                nxt[p] = score_dot(qs, p, prefetch)
        for p in groups[1]:
            att, floor = weights(p, later1)
            floors.append(floor)
            vals[p] = value_dot(p, att)
        for p in groups[0]:
            acc_ref[p] += vals[p]
        if prefetch is not None:
            for p in groups[1]:
                z_ref[1 - slot, p] = nxt[p]
        for p in groups[1]:
            acc_ref[p] += vals[p]
        floor_ref[0] = functools.reduce(jnp.minimum, floors)

    def finish():
        y = jnp.concatenate(
            [jnp.where(lane < HEAD_DIM, acc_ref[p, 0:TQ, :], acc_ref[p, TQ:, :]) for p in range(n_pairs)], axis=1)
        o_ref[0] = _rms(y, og_ref[...]).astype(BF16)

    @pl.when(jd == 0)
    def _():
        qs = start()
        scores(qs, 0, 0)
        step(qs, 0, 0, True, None)
        finish()

    @pl.when(jd > 0)
    def _():
        qs = start()
        scores(qs, jd, 0)
        step(qs, jd, 0, True, jd - 1)
        step(qs, jd - 1, 1, False, None)
        finish()

        @pl.when(jnp.logical_and(jd >= 2, floor_ref[0] <= SKIP_CARRY))
        def _():
            scores(qs, jd - 2, 1)

            def two_steps(state):
                j, _ = state
                step(qs, j, 1, False, jnp.maximum(j - 1, 0))

                @pl.when(jnp.logical_and(j >= 1, floor_ref[0] <= SKIP_CARRY))
                def _():
                    step(qs, j - 1, 0, False, jnp.maximum(j - 2, 0))

                return j - 2, floor_ref[0]

            lax.while_loop(lambda state: jnp.logical_and(state[0] >= 0, state[1] <= SKIP_CARRY),
                           two_steps, (jd - 2, floor_ref[0]))
            finish()


def _out_ffn_kernel(x_ref, ya_ref, ysb_ref, wout_ref, fg_ref, wgu_ref, wd_ref, o_ref):
    half = ya_ref.shape[1]
    x1 = (x_ref[...]
          + jnp.dot(ya_ref[...], wout_ref[0:half, :], preferred_element_type=F32)
          + jnp.dot(ysb_ref[...], wout_ref[half:, :], preferred_element_type=F32))
    h = _rms(x1, fg_ref[...]).astype(BF16)
    gu = jnp.dot(h, wgu_ref[...], preferred_element_type=F32)
    gate = gu[:, 0:FFN_HIDDEN]
    act = (gate * jax.nn.sigmoid(gate) * gu[:, FFN_HIDDEN:]).astype(BF16)
    o_ref[...] = x1 + jnp.dot(act, wd_ref[...], preferred_element_type=F32)


def _const_spec(shape):
    nd = len(shape)
    return pl.BlockSpec(shape, lambda *_: (0,) * nd, pipeline_mode=pl.Buffered(1))


def _slab_rows(rows, steps):
    n = steps
    while rows % n or (rows // n) % BF16_SUBLANES:
        n //= 2
    return rows // n


def _mix_in(x2, w_in, consts):
    n = x2.shape[0]
    tm = TM_IN
    row = lambda w: pl.BlockSpec((tm, w), lambda i: (i, 0))
    g, rest = consts[0], consts[1:]
    return pl.pallas_call(
        _mix_in_kernel,
        grid=(n // tm,),
        in_specs=[row(D_MODEL), _const_spec(g.shape), _const_spec(w_in.shape)]
        + [_const_spec(c.shape) for c in rest],
        out_specs=[row(CONV_WIDTH + SG_WIDTH), row(SB_WIDTH), row(SB_WIDTH), row(SB_WIDTH)],
        out_shape=[jax.ShapeDtypeStruct((n, CONV_WIDTH + SG_WIDTH), BF16)]
        + [jax.ShapeDtypeStruct((n, SB_WIDTH), BF16)] * 3,
        scratch_shapes=[pltpu.VMEM((HALO + tm, CONV_WIDTH), F32)],
        compiler_params=pltpu.CompilerParams(
            dimension_semantics=("arbitrary",), vmem_limit_bytes=VMEM_LIMIT),
        name="mix_in",
    )(x2, g, w_in, *rest)


def _attn(q, k, v, u, og, casts):
    b, s, w = q.shape
    steps = b * (s // TQ)
    w_specs, wb_specs, wb_shapes = [], [], []
    for stack, layer in casts:
        rows, cols = stack.shape[1:]
        slab = _slab_rows(rows, steps)
        every = steps // (rows // slab)
        pick = lambda bi, qi, every=every: (bi * (s // TQ) + qi) // every
        w_specs.append(pl.BlockSpec((None, slab, cols), lambda bi, qi, pick=pick, layer=layer: (layer, pick(bi, qi), 0)))
        wb_specs.append(pl.BlockSpec((slab, cols), lambda bi, qi, pick=pick: (pick(bi, qi), 0)))
        wb_shapes.append(jax.ShapeDtypeStruct((rows, cols), BF16))
    out = pl.pallas_call(
        functools.partial(_attn_kernel, n_cast=len(casts)),
        grid=(b, s // TQ),
        in_specs=[pl.BlockSpec((1, TQ, w), lambda bi, qi: (bi, qi, 0)),
                  pl.BlockSpec((1, s, w), lambda bi, qi: (bi, 0, 0)),
                  pl.BlockSpec((1, s, w), lambda bi, qi: (bi, 0, 0)),
                  _const_spec(u.shape), _const_spec(og.shape)] + w_specs,
        out_specs=[pl.BlockSpec((1, TQ, w), lambda bi, qi: (bi, qi, 0))] + wb_specs,
        out_shape=[jax.ShapeDtypeStruct((b, s, w), BF16)] + wb_shapes,
        scratch_shapes=[pltpu.VMEM((SB_HEADS // 2, 2 * TQ, LANES), F32),
                        pltpu.VMEM((SB_HEADS // 2, 2 * TQ, LANES), F32),
                        pltpu.VMEM((2, SB_HEADS // 2, 2 * TQ, TK), F32),
                        pltpu.SMEM((1,), F32)],
        compiler_params=pltpu.CompilerParams(
            dimension_semantics=("arbitrary", "arbitrary"), vmem_limit_bytes=VMEM_LIMIT),
        name="sb_attn",
    )(q, k, v, u, og, *[stack for stack, _ in casts])
    return out[0], out[1:]


def _out_ffn(x2, ya, ysb, wout, fg, wgu, wd):
    n = x2.shape[0]
    tm = TM_FFN
    row = lambda w: pl.BlockSpec((tm, w), lambda i: (i, 0))
    return pl.pallas_call(
        _out_ffn_kernel,
        grid=(n // tm,),
        in_specs=[row(D_MODEL), row(ya.shape[1]), row(ysb.shape[1]),
                  _const_spec(wout.shape), _const_spec(fg.shape), _const_spec(wgu.shape), _const_spec(wd.shape)],
        out_specs=row(D_MODEL),
        out_shape=jax.ShapeDtypeStruct((n, D_MODEL), F32),
        compiler_params=pltpu.CompilerParams(
            dimension_semantics=("parallel",), vmem_limit_bytes=VMEM_LIMIT),
        name="out_ffn",
    )(x2, ya, ysb, wout, fg, wgu, wd)


def kernel(x, mix_norm_g, w_in, conv_w, conv_b, conv_ln_g, conv_ln_b, sg_ln_g, sg_ln_b, sg_w, sg_b,
           q_norm_g, k_norm_g, out_norm_g, w_out, ffn_norm_g, w_gate_up, w_down):
    bsz, seq, d = x.shape
    assert (seq, d) == (SEQ, D_MODEL) and seq % TM_IN == 0 and seq % TK == 0
    n = bsz * seq
    x2 = x.reshape(n, d)

    hid = jnp.arange(MXU_WIDTH) // HEAD_DIM
    bd = jnp.where(hid[:, None] == hid[None, :], 1.0 / HEAD_DIM, 0.0).astype(BF16)
    ki = jnp.arange(TK)
    u = (ki[:, None] >= ki[None, :]).astype(BF16)
    row2 = lambda a: a.reshape(1, -1)
    w_in_b = w_in[0].astype(BF16)

    for l in range(DEPTH):
        cw = jnp.pad(conv_w[l], ((0, HALO - CONV_KERNEL), (0, 0)))
        sgb = jnp.repeat(sg_b[l].T, HEAD_DIM, axis=1)
        qg = jnp.tile(q_norm_g[l], SB_HEADS).reshape(1, -1) * (HEAD_DIM ** -0.5)
        kg = jnp.tile(k_norm_g[l], SB_HEADS).reshape(1, -1)
        og = row2(out_norm_g[l])
        ya, q, k, v = _mix_in(
            x2, w_in_b,
            [row2(mix_norm_g[l]), cw, row2(conv_b[l]), row2(conv_ln_g[l]), row2(conv_ln_b[l]),
             row2(sg_ln_g[l]), row2(sg_ln_b[l]), sg_w[l], sgb, qg, kg, og[:, :CONV_WIDTH + SG_WIDTH], bd])
        shp = (bsz, seq, SB_WIDTH)
        casts = [(w_out, l), (w_gate_up, l), (w_down, l)] + ([(w_in, l + 1)] if l + 1 < DEPTH else [])
        ysb, cast = _attn(q.reshape(shp), k.reshape(shp), v.reshape(shp), u, og[:, CONV_WIDTH + SG_WIDTH:], casts)
        x2 = _out_ffn(x2, ya, ysb.reshape(n, SB_WIDTH), cast[0], row2(ffn_norm_g[l]), cast[1], cast[2])
        if l + 1 < DEPTH:
            w_in_b = cast[3]
    return x2.reshape(bsz, seq, d)
```

```python
import functools
import math

import jax
import jax.numpy as jnp
from jax import lax
from jax.experimental import pallas as pl
from jax.experimental.pallas import tpu as pltpu

F32 = jnp.float32
BF16 = jnp.bfloat16

D_MODEL = 1024
SEQ = 2048
DEPTH = 4
HEAD_DIM = 64
CONV_WIDTH = 256
SG_WIDTH = 256
SB_WIDTH = 512
SB_HEADS = SB_WIDTH // HEAD_DIM
SG_HEADS = SG_WIDTH // HEAD_DIM
CONV_KERNEL = 31
SG_CHUNK = 128
OFF_SG = 2 * CONV_WIDTH
OFF_SB = OFF_SG + 2 * SG_WIDTH
IN_WIDTH = OFF_SB + 3 * SB_WIDTH
FFN_HIDDEN = 2816
RMS_EPS = 1e-6
LN_EPS = 1e-5
LOG2E = math.log2(math.e)
SKIP_CARRY = 128.0

LANES = 128
SUBLANES = 8
BF16_SUBLANES = 16
MXU_WIDTH = 256
HALO = 32
TM_IN = 1024
TM_FFN = 512
TQ = 256
TK = 256
VMEM_LIMIT = 56 * 1024 * 1024


def _rms(x, g):
    return x * lax.rsqrt(jnp.mean(x * x, axis=-1, keepdims=True) + RMS_EPS) * g


def _ln(x, g, b):
    mu = jnp.mean(x, axis=-1, keepdims=True)
    xc = x - mu
    var = jnp.mean(xc * xc, axis=-1, keepdims=True)
    return xc * lax.rsqrt(var + LN_EPS) * g + b


def _mix_in_kernel(x_ref, g_ref, w_in_ref, cw_ref, cb_ref, clg_ref, clb_ref,
                   slg_ref, slb_ref, sgw_ref, sgb_ref, qg_ref, kg_ref, og_ref, bd_ref,
                   ya_ref, q_ref, k_ref, v_ref, hist_ref):
    tm = x_ref.shape[0]
    tiles_per_seq = SEQ // tm
    n_chunks = tm // SG_CHUNK

    @pl.when(pl.program_id(0) % tiles_per_seq == 0)
    def _():
        hist_ref[0:HALO, :] = jnp.zeros((HALO, CONV_WIDTH), F32)

    h = _rms(x_ref[...], g_ref[...]).astype(BF16)

    def project(lo, hi):
        return jnp.dot(h, w_in_ref[:, lo:hi], preferred_element_type=F32)

    pa = project(0, OFF_SG)
    hist_ref[HALO:HALO + tm, :] = pa[:, 0:CONV_WIDTH] * jax.nn.sigmoid(pa[:, CONV_WIDTH:])
    pb = project(OFF_SG, OFF_SB)
    first = HALO - (CONV_KERNEL - 1)
    win = SG_CHUNK + HALO
    for c in range(n_chunks):
        r0 = c * SG_CHUNK
        window = hist_ref[r0:r0 + win, :]
        conv = jnp.zeros((SG_CHUNK, CONV_WIDTH), F32) + cb_ref[...]
        for b in range(SUBLANES):
            taps = [j for j in range(CONV_KERNEL) if (first + j) % SUBLANES == b]
            sh = window if b == 0 else pltpu.roll(window, win - b, 0)
            for j in taps:
                a0 = first + j - b
                conv = conv + sh[a0:a0 + SG_CHUNK, :] * cw_ref[j:j + 1, :]
        yc = _ln(conv, clg_ref[...], clb_ref[...])
        yc = yc * jax.nn.sigmoid(yc)
        ya_ref[r0:r0 + SG_CHUNK, 0:CONV_WIDTH] = _rms(yc, og_ref[:, 0:CONV_WIDTH]).astype(BF16)
    hist_ref[0:HALO, :] = hist_ref[tm:tm + HALO, :]

    pc = project(OFF_SB, IN_WIDTH)
    row = lax.broadcasted_iota(jnp.int32, (SG_CHUNK, SG_CHUNK), 0)
    col = lax.broadcasted_iota(jnp.int32, (SG_CHUNK, SG_CHUNK), 1)
    lane = lax.broadcasted_iota(jnp.int32, (SG_CHUNK, SG_WIDTH), 1)
    ws = [jnp.where(row >= col, sgw_ref[hd], 0.0).astype(BF16) for hd in range(SG_HEADS)]
    for c in range(n_chunks):
        rows = slice(c * SG_CHUNK, (c + 1) * SG_CHUNK)
        uv = pb[rows, :]
        uv = 0.5 * uv * (1.0 + lax.erf(uv * (1.0 / math.sqrt(2.0))))
        vc = _ln(uv[:, SG_WIDTH:], slg_ref[...], slb_ref[...]).astype(BF16)
        mixed = jnp.dot(ws[SG_HEADS - 1], vc, preferred_element_type=F32)
        for hd in range(SG_HEADS - 2, -1, -1):
            mixed = jnp.where(lane < (hd + 1) * HEAD_DIM,
                              jnp.dot(ws[hd], vc, preferred_element_type=F32), mixed)
        ysg = uv[:, 0:SG_WIDTH] * (mixed + sgb_ref[...])
        ya_ref[rows, CONV_WIDTH:] = _rms(ysg, og_ref[:, CONV_WIDTH:]).astype(BF16)

    def head_norm(t, g):
        sq = (t * t).astype(BF16)
        w = bd_ref.shape[0]
        ms = jnp.concatenate([jnp.dot(sq[:, c:c + w], bd_ref[...], preferred_element_type=F32)
                              for c in range(0, SB_WIDTH, w)], axis=1)
        return t * lax.rsqrt(ms + RMS_EPS) * g

    q_ref[...] = head_norm(pc[:, 0:SB_WIDTH], qg_ref[...]).astype(BF16)
    k_ref[...] = head_norm(pc[:, SB_WIDTH:2 * SB_WIDTH], kg_ref[...]).astype(BF16)
    v_ref[...] = pc[:, 2 * SB_WIDTH:].astype(BF16)


def _attn_kernel(q_ref, k_ref, v_ref, u_ref, og_ref, *refs, n_cast):
    w_refs, o_ref, wb_refs = refs[:n_cast], refs[n_cast], refs[n_cast + 1:2 * n_cast + 1]
    acc_ref, carry_ref, z_ref, floor_ref = refs[2 * n_cast + 1:]
    for w_ref, wb_ref in zip(w_refs, wb_refs):
        wb_ref[...] = w_ref[...].astype(BF16)

    n_pairs = SB_HEADS // 2
    half = n_pairs // 2
    qi = pl.program_id(1)
    jd = qi // (TK // TQ)
    lane = lax.broadcasted_iota(jnp.int32, (TQ, LANES), 1)

    def start():
        acc_ref[...] = jnp.zeros_like(acc_ref)
        carry_ref[...] = jnp.zeros_like(carry_ref)
        qs = []
        for p in range(n_pairs):
            qp = q_ref[0, :, p * LANES:(p + 1) * LANES]
            qs.append(jnp.concatenate([jnp.where(lane < HEAD_DIM, qp, jnp.zeros_like(qp)),
                                       jnp.where(lane >= HEAD_DIM, qp, jnp.zeros_like(qp))], axis=0))
        return qs

    def score_dot(qs, p, j):
        k0 = pl.multiple_of(j * TK, TK)
        return lax.dot_general(qs[p], k_ref[0, pl.ds(k0, TK), p * LANES:(p + 1) * LANES],
                               (((1,), (1,)), ((), ())), preferred_element_type=F32)

    def scores(qs, j, slot):
        for p in range(n_pairs):
            z_ref[slot, p] = score_dot(qs, p, j)

    def step(qs, j, slot, masked, prefetch):
        k0 = pl.multiple_of(j * TK, TK)
        hq = TQ // 2
        if masked:
            tri = (lax.broadcasted_iota(jnp.int32, (hq, hq), 1) < lax.broadcasted_iota(jnp.int32, (hq, hq), 0))

        def quadrants(f):
            zero = jnp.zeros((hq, hq), F32)
            rows = []
            for b0 in (0, TQ):
                rows.append(jnp.concatenate([jnp.where(tri, f(b0, 0), 0.0), zero], axis=1))
                rows.append(jnp.concatenate([f(b0 + hq, 0), jnp.where(tri, f(b0 + hq, hq), 0.0)], axis=1))
            return jnp.concatenate(rows, axis=0)

        def softplus(p):
            def f(r0, c0):
                z = z_ref[slot, p, r0:r0 + hq, c0:c0 + hq] if masked else z_ref[slot, p]
                return jnp.maximum(z, 0.0) + jnp.log(1.0 + jnp.exp2(jnp.abs(z) * -LOG2E))
            return (quadrants(f) if masked else f(0, 0)).astype(BF16)

        def later_dot(sps):
            return jnp.dot(jnp.concatenate(sps, axis=0), u_ref[...], preferred_element_type=F32)

        def weights(p, later):
            lat = later[(p % half) * 2 * TQ:(p % half + 1) * 2 * TQ, :]
            carry = carry_ref[p]
            if masked:
                att = quadrants(lambda r0, c0: jnp.exp(z_ref[slot, p, r0:r0 + hq, c0:c0 + hq]
                                                       - lat[r0:r0 + hq, c0:c0 + hq] - carry[r0:r0 + hq, :]))
            else:
                att = jnp.exp(z_ref[slot, p] - lat - jnp.concatenate([carry] * (TK // LANES), axis=1))
            carry = carry + lat[:, 0:1]
            carry_ref[p] = carry
            return att.astype(BF16), jnp.min(carry)

        def value_dot(p, att):
            return jnp.dot(att, v_ref[0, pl.ds(k0, TK), p * LANES:(p + 1) * LANES], preferred_element_type=F32)

        groups = [list(range(g * half, (g + 1) * half)) for g in range(2)]
        later0 = later_dot([softplus(p) for p in groups[0]])
        later1 = later_dot([softplus(p) for p in groups[1]])
        nxt, vals, floors = {}, {}, []
        if prefetch is not None:
            for p in groups[0]:
                nxt[p] = score_dot(qs, p, prefetch)
        for p in groups[0]:
            att, floor = weights(p, later0)
            floors.append(floor)
            vals[p] = value_dot(p, att)
        if prefetch is not None:
            for p in groups[0]:
                z_ref[1 - slot, p] = nxt[p]
            for p in groups[1]:
                nxt[p] = score_dot(qs, p, prefetch)
        for p in groups[1]:
            att, floor = weights(p, later1)
            floors.append(floor)
            vals[p] = value_dot(p, att)
        for p in groups[0]:
            acc_ref[p] += vals[p]
        if prefetch is not None:
            for p in groups[1]:
                z_ref[1 - slot, p] = nxt[p]
        for p in groups[1]:
            acc_ref[p] += vals[p]
        floor_ref[0] = functools.reduce(jnp.minimum, floors)

    def finish():
        y = jnp.concatenate(
            [jnp.where(lane < HEAD_DIM, acc_ref[p, 0:TQ, :], acc_ref[p, TQ:, :]) for p in range(n_pairs)], axis=1)
        o_ref[0] = _rms(y, og_ref[...]).astype(BF16)

    @pl.when(jd == 0)
    def _():
        qs = start()
        scores(qs, 0, 0)
        step(qs, 0, 0, True, None)
        finish()

    @pl.when(jd > 0)
    def _():
        qs = start()
        scores(qs, jd, 0)
        step(qs, jd, 0, True, jd - 1)
        step(qs, jd - 1, 1, False, None)
        finish()

        @pl.when(jnp.logical_and(jd >= 2, floor_ref[0] <= SKIP_CARRY))
        def _():
            scores(qs, jd - 2, 1)

            def two_steps(state):
                j, _ = state
                step(qs, j, 1, False, jnp.maximum(j - 1, 0))

                @pl.when(jnp.logical_and(j >= 1, floor_ref[0] <= SKIP_CARRY))
                def _():
                    step(qs, j - 1, 0, False, jnp.maximum(j - 2, 0))

                return j - 2, floor_ref[0]

            lax.while_loop(lambda state: jnp.logical_and(state[0] >= 0, state[1] <= SKIP_CARRY),
                           two_steps, (jd - 2, floor_ref[0]))
            finish()


def _out_ffn_kernel(x_ref, ya_ref, ysb_ref, wout_ref, fg_ref, wgu_ref, wd_ref, o_ref):
    half = ya_ref.shape[1]
    x1 = (x_ref[...]
          + jnp.dot(ya_ref[...], wout_ref[0:half, :], preferred_element_type=F32)
          + jnp.dot(ysb_ref[...], wout_ref[half:, :], preferred_element_type=F32))
    h = _rms(x1, fg_ref[...]).astype(BF16)
    gu = jnp.dot(h, wgu_ref[...], preferred_element_type=F32)
    gate = gu[:, 0:FFN_HIDDEN]
    act = (gate * jax.nn.sigmoid(gate) * gu[:, FFN_HIDDEN:]).astype(BF16)
    o_ref[...] = x1 + jnp.dot(act, wd_ref[...], preferred_element_type=F32)


def _const_spec(shape):
    nd = len(shape)
    return pl.BlockSpec(shape, lambda *_: (0,) * nd, pipeline_mode=pl.Buffered(1))


def _slab_rows(rows, steps):
    n = steps
    while rows % n or (rows // n) % BF16_SUBLANES:
        n //= 2
    return rows // n


def _mix_in(x2, w_in, consts):
    n = x2.shape[0]
    tm = TM_IN
    row = lambda w: pl.BlockSpec((tm, w), lambda i: (i, 0))
    g, rest = consts[0], consts[1:]
    return pl.pallas_call(
        _mix_in_kernel,
        grid=(n // tm,),
        in_specs=[row(D_MODEL), _const_spec(g.shape), _const_spec(w_in.shape)]
        + [_const_spec(c.shape) for c in rest],
        out_specs=[row(CONV_WIDTH + SG_WIDTH), row(SB_WIDTH), row(SB_WIDTH), row(SB_WIDTH)],
        out_shape=[jax.ShapeDtypeStruct((n, CONV_WIDTH + SG_WIDTH), BF16)]
        + [jax.ShapeDtypeStruct((n, SB_WIDTH), BF16)] * 3,
        scratch_shapes=[pltpu.VMEM((HALO + tm, CONV_WIDTH), F32)],
        compiler_params=pltpu.CompilerParams(
            dimension_semantics=("arbitrary",), vmem_limit_bytes=VMEM_LIMIT),
        name="mix_in",
    )(x2, g, w_in, *rest)


def _attn(q, k, v, u, og, casts):
    b, s, w = q.shape
    steps = b * (s // TQ)
    w_specs, wb_specs, wb_shapes = [], [], []
    for stack, layer in casts:
        rows, cols = stack.shape[1:]
        slab = _slab_rows(rows, steps)
        every = steps // (rows // slab)
        pick = lambda bi, qi, every=every: (bi * (s // TQ) + qi) // every
        w_specs.append(pl.BlockSpec((None, slab, cols), lambda bi, qi, pick=pick, layer=layer: (layer, pick(bi, qi), 0)))
        wb_specs.append(pl.BlockSpec((slab, cols), lambda bi, qi, pick=pick: (pick(bi, qi), 0)))
        wb_shapes.append(jax.ShapeDtypeStruct((rows, cols), BF16))
    out = pl.pallas_call(
        functools.partial(_attn_kernel, n_cast=len(casts)),
        grid=(b, s // TQ),
        in_specs=[pl.BlockSpec((1, TQ, w), lambda bi, qi: (bi, qi, 0)),
                  pl.BlockSpec((1, s, w), lambda bi, qi: (bi, 0, 0)),
                  pl.BlockSpec((1, s, w), lambda bi, qi: (bi, 0, 0)),
                  _const_spec(u.shape), _const_spec(og.shape)] + w_specs,
        out_specs=[pl.BlockSpec((1, TQ, w), lambda bi, qi: (bi, qi, 0))] + wb_specs,
        out_shape=[jax.ShapeDtypeStruct((b, s, w), BF16)] + wb_shapes,
        scratch_shapes=[pltpu.VMEM((SB_HEADS // 2, 2 * TQ, LANES), F32),
                        pltpu.VMEM((SB_HEADS // 2, 2 * TQ, LANES), F32),
                        pltpu.VMEM((2, SB_HEADS // 2, 2 * TQ, TK), F32),
                        pltpu.SMEM((1,), F32)],
        compiler_params=pltpu.CompilerParams(
            dimension_semantics=("arbitrary", "arbitrary"), vmem_limit_bytes=VMEM_LIMIT),
        name="sb_attn",
    )(q, k, v, u, og, *[stack for stack, _ in casts])
    return out[0], out[1:]


def _out_ffn(x2, ya, ysb, wout, fg, wgu, wd):
    n = x2.shape[0]
    tm = TM_FFN
    row = lambda w: pl.BlockSpec((tm, w), lambda i: (i, 0))
    return pl.pallas_call(
        _out_ffn_kernel,
        grid=(n // tm,),
        in_specs=[row(D_MODEL), row(ya.shape[1]), row(ysb.shape[1]),
                  _const_spec(wout.shape), _const_spec(fg.shape), _const_spec(wgu.shape), _const_spec(wd.shape)],
        out_specs=row(D_MODEL),
        out_shape=jax.ShapeDtypeStruct((n, D_MODEL), F32),
        compiler_params=pltpu.CompilerParams(
            dimension_semantics=("parallel",), vmem_limit_bytes=VMEM_LIMIT),
        name="out_ffn",
    )(x2, ya, ysb, wout, fg, wgu, wd)


def kernel(x, mix_norm_g, w_in, conv_w, conv_b, conv_ln_g, conv_ln_b, sg_ln_g, sg_ln_b, sg_w, sg_b,
           q_norm_g, k_norm_g, out_norm_g, w_out, ffn_norm_g, w_gate_up, w_down):
    bsz, seq, d = x.shape
    assert (seq, d) == (SEQ, D_MODEL) and seq % TM_IN == 0 and seq % TK == 0
    assert TQ == TK and TQ // 2 == LANES
    n = bsz * seq
    x2 = x.reshape(n, d)

    hid = jnp.arange(MXU_WIDTH) // HEAD_DIM
    bd = jnp.where(hid[:, None] == hid[None, :], 1.0 / HEAD_DIM, 0.0).astype(BF16)
    ki = jnp.arange(TK)
    u = (ki[:, None] >= ki[None, :]).astype(BF16)
    row2 = lambda a: a.reshape(1, -1)
    w_in_b = w_in[0].astype(BF16)

    for l in range(DEPTH):
        cw = jnp.pad(conv_w[l], ((0, HALO - CONV_KERNEL), (0, 0)))
        sgb = jnp.repeat(sg_b[l].T, HEAD_DIM, axis=1)
        qg = jnp.tile(q_norm_g[l], SB_HEADS).reshape(1, -1) * (HEAD_DIM ** -0.5)
        kg = jnp.tile(k_norm_g[l], SB_HEADS).reshape(1, -1)
        og = row2(out_norm_g[l])
        ya, q, k, v = _mix_in(
            x2, w_in_b,
            [row2(mix_norm_g[l]), cw, row2(conv_b[l]), row2(conv_ln_g[l]), row2(conv_ln_b[l]),
             row2(sg_ln_g[l]), row2(sg_ln_b[l]), sg_w[l], sgb, qg, kg, og[:, :CONV_WIDTH + SG_WIDTH], bd])
        shp = (bsz, seq, SB_WIDTH)
        casts = [(w_out, l), (w_gate_up, l), (w_down, l)] + ([(w_in, l + 1)] if l + 1 < DEPTH else [])
        ysb, cast = _attn(q.reshape(shp), k.reshape(shp), v.reshape(shp), u, og[:, CONV_WIDTH + SG_WIDTH:], casts)
        x2 = _out_ffn(x2, ya, ysb.reshape(n, SB_WIDTH), cast[0], row2(ffn_norm_g[l]), cast[1], cast[2])
        if l + 1 < DEPTH:
            w_in_b = cast[3]
    return x2.reshape(bsz, seq, d)
```

```python
import functools
import math

import jax
import jax.numpy as jnp
from jax import lax
from jax.experimental import pallas as pl
from jax.experimental.pallas import tpu as pltpu

F32 = jnp.float32
BF16 = jnp.bfloat16

D_MODEL = 1024
SEQ = 2048
DEPTH = 4
HEAD_DIM = 64
CONV_WIDTH = 256
SG_WIDTH = 256
SB_WIDTH = 512
SB_HEADS = SB_WIDTH // HEAD_DIM
SG_HEADS = SG_WIDTH // HEAD_DIM
CONV_KERNEL = 31
SG_CHUNK = 128
OFF_SG = 2 * CONV_WIDTH
OFF_SB = OFF_SG + 2 * SG_WIDTH
IN_WIDTH = OFF_SB + 3 * SB_WIDTH
FFN_HIDDEN = 2816
RMS_EPS = 1e-6
LN_EPS = 1e-5
LOG2E = math.log2(math.e)
SKIP_CARRY = 128.0

LANES = 128
SUBLANES = 8
BF16_SUBLANES = 16
MXU_WIDTH = 256
HALO = 32
TM_IN = 1024
TM_FFN = 512
TQ = 256
Q_PER_STEP = 2
TK = 256
VMEM_LIMIT = 56 * 1024 * 1024


def _rms(x, g):
    return x * lax.rsqrt(jnp.mean(x * x, axis=-1, keepdims=True) + RMS_EPS) * g


def _ln(x, g, b):
    mu = jnp.mean(x, axis=-1, keepdims=True)
    xc = x - mu
    var = jnp.mean(xc * xc, axis=-1, keepdims=True)
    return xc * lax.rsqrt(var + LN_EPS) * g + b


def _mix_in_kernel(x_ref, g_ref, w_in_ref, cw_ref, cb_ref, clg_ref, clb_ref,
                   slg_ref, slb_ref, sgw_ref, sgb_ref, qg_ref, kg_ref, og_ref, bd_ref,
                   ya_ref, q_ref, k_ref, v_ref, hist_ref):
    tm = x_ref.shape[0]
    tiles_per_seq = SEQ // tm
    n_chunks = tm // SG_CHUNK

    @pl.when(pl.program_id(0) % tiles_per_seq == 0)
    def _():
        hist_ref[0:HALO, :] = jnp.zeros((HALO, CONV_WIDTH), F32)

    h = _rms(x_ref[...], g_ref[...]).astype(BF16)

    def project(lo, hi):
        return jnp.dot(h, w_in_ref[:, lo:hi], preferred_element_type=F32)

    pa = project(0, OFF_SG)
    hist_ref[HALO:HALO + tm, :] = pa[:, 0:CONV_WIDTH] * jax.nn.sigmoid(pa[:, CONV_WIDTH:])
    pb = project(OFF_SG, OFF_SB)
    first = HALO - (CONV_KERNEL - 1)
    win = SG_CHUNK + HALO
    for c in range(n_chunks):
        r0 = c * SG_CHUNK
        window = hist_ref[r0:r0 + win, :]
        conv = jnp.zeros((SG_CHUNK, CONV_WIDTH), F32) + cb_ref[...]
        for b in range(SUBLANES):
            taps = [j for j in range(CONV_KERNEL) if (first + j) % SUBLANES == b]
            sh = window if b == 0 else pltpu.roll(window, win - b, 0)
            for j in taps:
                a0 = first + j - b
                conv = conv + sh[a0:a0 + SG_CHUNK, :] * cw_ref[j:j + 1, :]
        yc = _ln(conv, clg_ref[...], clb_ref[...])
        yc = yc * jax.nn.sigmoid(yc)
        ya_ref[r0:r0 + SG_CHUNK, 0:CONV_WIDTH] = _rms(yc, og_ref[:, 0:CONV_WIDTH]).astype(BF16)
    hist_ref[0:HALO, :] = hist_ref[tm:tm + HALO, :]

    pc = project(OFF_SB, IN_WIDTH)
    row = lax.broadcasted_iota(jnp.int32, (SG_CHUNK, SG_CHUNK), 0)
    col = lax.broadcasted_iota(jnp.int32, (SG_CHUNK, SG_CHUNK), 1)
    lane = lax.broadcasted_iota(jnp.int32, (SG_CHUNK, SG_WIDTH), 1)
    ws = [jnp.where(row >= col, sgw_ref[hd], 0.0).astype(BF16) for hd in range(SG_HEADS)]
    for c in range(n_chunks):
        rows = slice(c * SG_CHUNK, (c + 1) * SG_CHUNK)
        uv = pb[rows, :]
        uv = 0.5 * uv * (1.0 + lax.erf(uv * (1.0 / math.sqrt(2.0))))
        vc = _ln(uv[:, SG_WIDTH:], slg_ref[...], slb_ref[...]).astype(BF16)
        mixed = jnp.dot(ws[SG_HEADS - 1], vc, preferred_element_type=F32)
        for hd in range(SG_HEADS - 2, -1, -1):
            mixed = jnp.where(lane < (hd + 1) * HEAD_DIM,
                              jnp.dot(ws[hd], vc, preferred_element_type=F32), mixed)
        ysg = uv[:, 0:SG_WIDTH] * (mixed + sgb_ref[...])
        ya_ref[rows, CONV_WIDTH:] = _rms(ysg, og_ref[:, CONV_WIDTH:]).astype(BF16)

    def head_norm(t, g):
        sq = (t * t).astype(BF16)
        w = bd_ref.shape[0]
        ms = jnp.concatenate([jnp.dot(sq[:, c:c + w], bd_ref[...], preferred_element_type=F32)
                              for c in range(0, SB_WIDTH, w)], axis=1)
        return t * lax.rsqrt(ms + RMS_EPS) * g

    q_ref[...] = head_norm(pc[:, 0:SB_WIDTH], qg_ref[...]).astype(BF16)
    k_ref[...] = head_norm(pc[:, SB_WIDTH:2 * SB_WIDTH], kg_ref[...]).astype(BF16)
    v_ref[...] = pc[:, 2 * SB_WIDTH:].astype(BF16)


def _attn_kernel(q_ref, k_ref, v_ref, u_ref, og_ref, *refs, n_cast):
    w_refs, o_ref, wb_refs = refs[:n_cast], refs[n_cast], refs[n_cast + 1:2 * n_cast + 1]
    acc_ref, carry_ref, z_ref, floor_ref = refs[2 * n_cast + 1:]
    for w_ref, wb_ref in zip(w_refs, wb_refs):
        wb_ref[...] = w_ref[...].astype(BF16)

    n_pairs = SB_HEADS // 2
    half = n_pairs // 2
    lane = lax.broadcasted_iota(jnp.int32, (TQ, LANES), 1)

    def one_block(sub):
        qi = pl.program_id(1) * Q_PER_STEP + sub
        jd = qi // (TK // TQ)
        q_rows = slice(sub * TQ, (sub + 1) * TQ)

        def start():
            acc_ref[...] = jnp.zeros_like(acc_ref)
            carry_ref[...] = jnp.zeros_like(carry_ref)
            qs = []
            for p in range(n_pairs):
                qp = q_ref[0, q_rows, p * LANES:(p + 1) * LANES]
                qs.append(jnp.concatenate([jnp.where(lane < HEAD_DIM, qp, jnp.zeros_like(qp)),
                                           jnp.where(lane >= HEAD_DIM, qp, jnp.zeros_like(qp))], axis=0))
            return qs

        def score_dot(qs, p, j):
            k0 = pl.multiple_of(j * TK, TK)
            return lax.dot_general(qs[p], k_ref[0, pl.ds(k0, TK), p * LANES:(p + 1) * LANES],
                                   (((1,), (1,)), ((), ())), preferred_element_type=F32)

        def scores(qs, j, slot):
            for p in range(n_pairs):
                z_ref[slot, p] = score_dot(qs, p, j)

        def step(qs, j, slot, masked, prefetch):
            k0 = pl.multiple_of(j * TK, TK)
            hq = TQ // 2
            if masked:
                tri = (lax.broadcasted_iota(jnp.int32, (hq, hq), 1) < lax.broadcasted_iota(jnp.int32, (hq, hq), 0))

            def quadrants(f):
                zero = jnp.zeros((hq, hq), F32)
                rows = []
                for b0 in (0, TQ):
                    rows.append(jnp.concatenate([jnp.where(tri, f(b0, 0), 0.0), zero], axis=1))
                    rows.append(jnp.concatenate([f(b0 + hq, 0), jnp.where(tri, f(b0 + hq, hq), 0.0)], axis=1))
                return jnp.concatenate(rows, axis=0)

            def softplus(p):
                def f(r0, c0):
                    z = z_ref[slot, p, r0:r0 + hq, c0:c0 + hq] if masked else z_ref[slot, p]
                    return jnp.maximum(z, 0.0) + jnp.log(1.0 + jnp.exp2(jnp.abs(z) * -LOG2E))
                return (quadrants(f) if masked else f(0, 0)).astype(BF16)

            def later_dot(sps):
                return jnp.dot(jnp.concatenate(sps, axis=0), u_ref[...], preferred_element_type=F32)

            def weights(p, later):
                lat = later[(p % half) * 2 * TQ:(p % half + 1) * 2 * TQ, :]
                carry = carry_ref[p]
                if masked:
                    att = quadrants(lambda r0, c0: jnp.exp(z_ref[slot, p, r0:r0 + hq, c0:c0 + hq]
                                                           - lat[r0:r0 + hq, c0:c0 + hq] - carry[r0:r0 + hq, :]))
                else:
                    att = jnp.exp(z_ref[slot, p] - lat - jnp.concatenate([carry] * (TK // LANES), axis=1))
                carry = carry + lat[:, 0:1]
                carry_ref[p] = carry
                return att.astype(BF16), jnp.min(carry)

            def value_dot(p, att):
                return jnp.dot(att, v_ref[0, pl.ds(k0, TK), p * LANES:(p + 1) * LANES], preferred_element_type=F32)

            groups = [list(range(g * half, (g + 1) * half)) for g in range(2)]
            later0 = later_dot([softplus(p) for p in groups[0]])
            later1 = later_dot([softplus(p) for p in groups[1]])
            nxt, vals, floors = {}, {}, []
            if prefetch is not None:
                for p in groups[0]:
                    nxt[p] = score_dot(qs, p, prefetch)
            for p in groups[0]:
                att, floor = weights(p, later0)
                floors.append(floor)
                vals[p] = value_dot(p, att)
            if prefetch is not None:
                for p in groups[0]:
                    z_ref[1 - slot, p] = nxt[p]
                for p in groups[1]:
                    nxt[p] = score_dot(qs, p, prefetch)
            for p in groups[1]:
                att, floor = weights(p, later1)
                floors.append(floor)
                vals[p] = value_dot(p, att)
            for p in groups[0]:
                acc_ref[p] += vals[p]
            if prefetch is not None:
                for p in groups[1]:
                    z_ref[1 - slot, p] = nxt[p]
            for p in groups[1]:
                acc_ref[p] += vals[p]
            floor_ref[0] = functools.reduce(jnp.minimum, floors)

        def finish():
            y = jnp.concatenate(
                [jnp.where(lane < HEAD_DIM, acc_ref[p, 0:TQ, :], acc_ref[p, TQ:, :]) for p in range(n_pairs)], axis=1)
            o_ref[0, q_rows, :] = _rms(y, og_ref[...]).astype(BF16)

        def first_block():
            qs = start()
            scores(qs, 0, 0)
            step(qs, 0, 0, True, None)
            finish()

        def later_block():
            qs = start()
            scores(qs, jd, 0)
            step(qs, jd, 0, True, jd - 1)
            step(qs, jd - 1, 1, False, None)
            finish()

            @pl.when(jnp.logical_and(jd >= 2, floor_ref[0] <= SKIP_CARRY))
            def _():
                scores(qs, jd - 2, 1)

                def two_steps(state):
                    j, _ = state
                    step(qs, j, 1, False, jnp.maximum(j - 1, 0))

                    @pl.when(jnp.logical_and(j >= 1, floor_ref[0] <= SKIP_CARRY))
                    def _():
                        step(qs, j - 1, 0, False, jnp.maximum(j - 2, 0))

                    return j - 2, floor_ref[0]

                lax.while_loop(lambda state: jnp.logical_and(state[0] >= 0, state[1] <= SKIP_CARRY),
                               two_steps, (jd - 2, floor_ref[0]))
                finish()

        if sub == 0:
            pl.when(jd == 0)(first_block)
            pl.when(jd > 0)(later_block)
        else:
            later_block()

    for sub in range(Q_PER_STEP):
        one_block(sub)


def _out_ffn_kernel(x_ref, ya_ref, ysb_ref, wout_ref, fg_ref, wgu_ref, wd_ref, o_ref):
    half = ya_ref.shape[1]
    x1 = (x_ref[...]
          + jnp.dot(ya_ref[...], wout_ref[0:half, :], preferred_element_type=F32)
          + jnp.dot(ysb_ref[...], wout_ref[half:, :], preferred_element_type=F32))
    h = _rms(x1, fg_ref[...]).astype(BF16)
    gu = jnp.dot(h, wgu_ref[...], preferred_element_type=F32)
    gate = gu[:, 0:FFN_HIDDEN]
    act = (gate * jax.nn.sigmoid(gate) * gu[:, FFN_HIDDEN:]).astype(BF16)
    o_ref[...] = x1 + jnp.dot(act, wd_ref[...], preferred_element_type=F32)


def _const_spec(shape):
    nd = len(shape)
    return pl.BlockSpec(shape, lambda *_: (0,) * nd, pipeline_mode=pl.Buffered(1))


def _slab_rows(rows, steps):
    n = steps
    while rows % n or (rows // n) % BF16_SUBLANES:
        n //= 2
    return rows // n


def _mix_in(x2, w_in, consts):
    n = x2.shape[0]
    tm = TM_IN
    row = lambda w: pl.BlockSpec((tm, w), lambda i: (i, 0))
    g, rest = consts[0], consts[1:]
    return pl.pallas_call(
        _mix_in_kernel,
        grid=(n // tm,),
        in_specs=[row(D_MODEL), _const_spec(g.shape), _const_spec(w_in.shape)]
        + [_const_spec(c.shape) for c in rest],
        out_specs=[row(CONV_WIDTH + SG_WIDTH), row(SB_WIDTH), row(SB_WIDTH), row(SB_WIDTH)],
        out_shape=[jax.ShapeDtypeStruct((n, CONV_WIDTH + SG_WIDTH), BF16)]
        + [jax.ShapeDtypeStruct((n, SB_WIDTH), BF16)] * 3,
        scratch_shapes=[pltpu.VMEM((HALO + tm, CONV_WIDTH), F32)],
        compiler_params=pltpu.CompilerParams(
            dimension_semantics=("arbitrary",), vmem_limit_bytes=VMEM_LIMIT),
        name="mix_in",
    )(x2, g, w_in, *rest)


def _attn(q, k, v, u, og, casts):
    b, s, w = q.shape
    tq = Q_PER_STEP * TQ
    steps = b * (s // tq)
    w_specs, wb_specs, wb_shapes = [], [], []
    for stack, layer in casts:
        rows, cols = stack.shape[1:]
        slab = _slab_rows(rows, steps)
        every = steps // (rows // slab)
        pick = lambda bi, qi, every=every: (bi * (s // tq) + qi) // every
        w_specs.append(pl.BlockSpec((None, slab, cols), lambda bi, qi, pick=pick, layer=layer: (layer, pick(bi, qi), 0)))
        wb_specs.append(pl.BlockSpec((slab, cols), lambda bi, qi, pick=pick: (pick(bi, qi), 0)))
        wb_shapes.append(jax.ShapeDtypeStruct((rows, cols), BF16))
    out = pl.pallas_call(
        functools.partial(_attn_kernel, n_cast=len(casts)),
        grid=(b, s // tq),
        in_specs=[pl.BlockSpec((1, tq, w), lambda bi, qi: (bi, qi, 0)),
                  pl.BlockSpec((1, s, w), lambda bi, qi: (bi, 0, 0)),
                  pl.BlockSpec((1, s, w), lambda bi, qi: (bi, 0, 0)),
                  _const_spec(u.shape), _const_spec(og.shape)] + w_specs,
        out_specs=[pl.BlockSpec((1, tq, w), lambda bi, qi: (bi, qi, 0))] + wb_specs,
        out_shape=[jax.ShapeDtypeStruct((b, s, w), BF16)] + wb_shapes,
        scratch_shapes=[pltpu.VMEM((SB_HEADS // 2, 2 * TQ, LANES), F32),
                        pltpu.VMEM((SB_HEADS // 2, 2 * TQ, LANES), F32),
                        pltpu.VMEM((2, SB_HEADS // 2, 2 * TQ, TK), F32),
                        pltpu.SMEM((1,), F32)],
        compiler_params=pltpu.CompilerParams(
            dimension_semantics=("arbitrary", "arbitrary"), vmem_limit_bytes=VMEM_LIMIT),
        name="sb_attn",
    )(q, k, v, u, og, *[stack for stack, _ in casts])
    return out[0], out[1:]


def _out_ffn(x2, ya, ysb, wout, fg, wgu, wd):
    n = x2.shape[0]
    tm = TM_FFN
    row = lambda w: pl.BlockSpec((tm, w), lambda i: (i, 0))
    return pl.pallas_call(
        _out_ffn_kernel,
        grid=(n // tm,),
        in_specs=[row(D_MODEL), row(ya.shape[1]), row(ysb.shape[1]),
                  _const_spec(wout.shape), _const_spec(fg.shape), _const_spec(wgu.shape), _const_spec(wd.shape)],
        out_specs=row(D_MODEL),
        out_shape=jax.ShapeDtypeStruct((n, D_MODEL), F32),
        compiler_params=pltpu.CompilerParams(
            dimension_semantics=("parallel",), vmem_limit_bytes=VMEM_LIMIT),
        name="out_ffn",
    )(x2, ya, ysb, wout, fg, wgu, wd)


def kernel(x, mix_norm_g, w_in, conv_w, conv_b, conv_ln_g, conv_ln_b, sg_ln_g, sg_ln_b, sg_w, sg_b,
           q_norm_g, k_norm_g, out_norm_g, w_out, ffn_norm_g, w_gate_up, w_down):
    bsz, seq, d = x.shape
    assert (seq, d) == (SEQ, D_MODEL) and seq % TM_IN == 0 and seq % TK == 0
    assert TQ == TK and TQ // 2 == LANES
    n = bsz * seq
    x2 = x.reshape(n, d)

    hid = jnp.arange(MXU_WIDTH) // HEAD_DIM
    bd = jnp.where(hid[:, None] == hid[None, :], 1.0 / HEAD_DIM, 0.0).astype(BF16)
    ki = jnp.arange(TK)
    u = (ki[:, None] >= ki[None, :]).astype(BF16)
    row2 = lambda a: a.reshape(1, -1)
    w_in_b = w_in[0].astype(BF16)

    for l in range(DEPTH):
        cw = jnp.pad(conv_w[l], ((0, HALO - CONV_KERNEL), (0, 0)))
        sgb = jnp.repeat(sg_b[l].T, HEAD_DIM, axis=1)
        qg = jnp.tile(q_norm_g[l], SB_HEADS).reshape(1, -1) * (HEAD_DIM ** -0.5)
        kg = jnp.tile(k_norm_g[l], SB_HEADS).reshape(1, -1)
        og = row2(out_norm_g[l])
        ya, q, k, v = _mix_in(
            x2, w_in_b,
            [row2(mix_norm_g[l]), cw, row2(conv_b[l]), row2(conv_ln_g[l]), row2(conv_ln_b[l]),
             row2(sg_ln_g[l]), row2(sg_ln_b[l]), sg_w[l], sgb, qg, kg, og[:, :CONV_WIDTH + SG_WIDTH], bd])
        shp = (bsz, seq, SB_WIDTH)
        casts = [(w_out, l), (w_gate_up, l), (w_down, l)] + ([(w_in, l + 1)] if l + 1 < DEPTH else [])
        ysb, cast = _attn(q.reshape(shp), k.reshape(shp), v.reshape(shp), u, og[:, CONV_WIDTH + SG_WIDTH:], casts)
        x2 = _out_ffn(x2, ya, ysb.reshape(n, SB_WIDTH), cast[0], row2(ffn_norm_g[l]), cast[1], cast[2])
        if l + 1 < DEPTH:
            w_in_b = cast[3]
    return x2.reshape(bsz, seq, d)
```

```python
import functools
import math

import jax
import jax.numpy as jnp
from jax import lax
from jax.experimental import pallas as pl
from jax.experimental.pallas import tpu as pltpu

F32 = jnp.float32
BF16 = jnp.bfloat16

D_MODEL = 1024
SEQ = 2048
DEPTH = 4
HEAD_DIM = 64
CONV_WIDTH = 256
SG_WIDTH = 256
SB_WIDTH = 512
SB_HEADS = SB_WIDTH // HEAD_DIM
SG_HEADS = SG_WIDTH // HEAD_DIM
CONV_KERNEL = 31
SG_CHUNK = 128
OFF_SG = 2 * CONV_WIDTH
OFF_SB = OFF_SG + 2 * SG_WIDTH
IN_WIDTH = OFF_SB + 3 * SB_WIDTH
FFN_HIDDEN = 2816
RMS_EPS = 1e-6
LN_EPS = 1e-5
LOG2E = math.log2(math.e)
SKIP_CARRY = 128.0

LANES = 128
SUBLANES = 8
BF16_SUBLANES = 16
MXU_WIDTH = 256
HALO = 32
TM_IN = 1024
TM_FFN = 1024
FFN_ROWS = 512
TQ = 256
Q_PER_STEP = 2
TK = 256
VMEM_LIMIT = 56 * 1024 * 1024


def _rms(x, g):
    return x * lax.rsqrt(jnp.mean(x * x, axis=-1, keepdims=True) + RMS_EPS) * g


def _ln(x, g, b):
    mu = jnp.mean(x, axis=-1, keepdims=True)
    xc = x - mu
    var = jnp.mean(xc * xc, axis=-1, keepdims=True)
    return xc * lax.rsqrt(var + LN_EPS) * g + b


def _mix_in_kernel(x_ref, g_ref, w_in_ref, cw_ref, cb_ref, clg_ref, clb_ref,
                   slg_ref, slb_ref, sgw_ref, sgb_ref, qg_ref, kg_ref, og_ref, bd_ref,
                   ya_ref, q_ref, k_ref, v_ref, hist_ref):
    tm = x_ref.shape[0]
    tiles_per_seq = SEQ // tm
    n_chunks = tm // SG_CHUNK

    @pl.when(pl.program_id(0) % tiles_per_seq == 0)
    def _():
        hist_ref[0:HALO, :] = jnp.zeros((HALO, CONV_WIDTH), F32)

    h = _rms(x_ref[...], g_ref[...]).astype(BF16)

    def project(lo, hi):
        return jnp.dot(h, w_in_ref[:, lo:hi], preferred_element_type=F32)

    pa = project(0, OFF_SG)
    hist_ref[HALO:HALO + tm, :] = pa[:, 0:CONV_WIDTH] * jax.nn.sigmoid(pa[:, CONV_WIDTH:])
    pb = project(OFF_SG, OFF_SB)
    first = HALO - (CONV_KERNEL - 1)
    win = SG_CHUNK + HALO
    for c in range(n_chunks):
        r0 = c * SG_CHUNK
        window = hist_ref[r0:r0 + win, :]
        conv = jnp.zeros((SG_CHUNK, CONV_WIDTH), F32) + cb_ref[...]
        for b in range(SUBLANES):
            taps = [j for j in range(CONV_KERNEL) if (first + j) % SUBLANES == b]
            sh = window if b == 0 else pltpu.roll(window, win - b, 0)
            for j in taps:
                a0 = first + j - b
                conv = conv + sh[a0:a0 + SG_CHUNK, :] * cw_ref[j:j + 1, :]
        yc = _ln(conv, clg_ref[...], clb_ref[...])
        yc = yc * jax.nn.sigmoid(yc)
        ya_ref[r0:r0 + SG_CHUNK, 0:CONV_WIDTH] = _rms(yc, og_ref[:, 0:CONV_WIDTH]).astype(BF16)
    hist_ref[0:HALO, :] = hist_ref[tm:tm + HALO, :]

    pc = project(OFF_SB, IN_WIDTH)
    row = lax.broadcasted_iota(jnp.int32, (SG_CHUNK, SG_CHUNK), 0)
    col = lax.broadcasted_iota(jnp.int32, (SG_CHUNK, SG_CHUNK), 1)
    lane = lax.broadcasted_iota(jnp.int32, (SG_CHUNK, SG_WIDTH), 1)
    ws = [jnp.where(row >= col, sgw_ref[hd], 0.0).astype(BF16) for hd in range(SG_HEADS)]
    for c in range(n_chunks):
        rows = slice(c * SG_CHUNK, (c + 1) * SG_CHUNK)
        uv = pb[rows, :]
        uv = 0.5 * uv * (1.0 + lax.erf(uv * (1.0 / math.sqrt(2.0))))
        vc = _ln(uv[:, SG_WIDTH:], slg_ref[...], slb_ref[...]).astype(BF16)
        mixed = jnp.dot(ws[SG_HEADS - 1], vc, preferred_element_type=F32)
        for hd in range(SG_HEADS - 2, -1, -1):
            mixed = jnp.where(lane < (hd + 1) * HEAD_DIM,
                              jnp.dot(ws[hd], vc, preferred_element_type=F32), mixed)
        ysg = uv[:, 0:SG_WIDTH] * (mixed + sgb_ref[...])
        ya_ref[rows, CONV_WIDTH:] = _rms(ysg, og_ref[:, CONV_WIDTH:]).astype(BF16)

    def head_norm(t, g):
        sq = (t * t).astype(BF16)
        w = bd_ref.shape[0]
        ms = jnp.concatenate([jnp.dot(sq[:, c:c + w], bd_ref[...], preferred_element_type=F32)
                              for c in range(0, SB_WIDTH, w)], axis=1)
        return t * lax.rsqrt(ms + RMS_EPS) * g

    q_ref[...] = head_norm(pc[:, 0:SB_WIDTH], qg_ref[...]).astype(BF16)
    k_ref[...] = head_norm(pc[:, SB_WIDTH:2 * SB_WIDTH], kg_ref[...]).astype(BF16)
    v_ref[...] = pc[:, 2 * SB_WIDTH:].astype(BF16)


def _attn_kernel(q_ref, k_ref, v_ref, u_ref, og_ref, *refs, n_cast):
    w_refs, o_ref, wb_refs = refs[:n_cast], refs[n_cast], refs[n_cast + 1:2 * n_cast + 1]
    acc_ref, carry_ref, z_ref, floor_ref = refs[2 * n_cast + 1:]
    for w_ref, wb_ref in zip(w_refs, wb_refs):
        wb_ref[...] = w_ref[...].astype(BF16)

    n_pairs = SB_HEADS // 2
    half = n_pairs // 2
    lane = lax.broadcasted_iota(jnp.int32, (TQ, LANES), 1)

    def one_block(sub):
        qi = pl.program_id(1) * Q_PER_STEP + sub
        jd = qi // (TK // TQ)
        q_rows = slice(sub * TQ, (sub + 1) * TQ)

        def start():
            acc_ref[...] = jnp.zeros_like(acc_ref)
            carry_ref[...] = jnp.zeros_like(carry_ref)
            qs = []
            for p in range(n_pairs):
                qp = q_ref[0, q_rows, p * LANES:(p + 1) * LANES]
                qs.append(jnp.concatenate([jnp.where(lane < HEAD_DIM, qp, jnp.zeros_like(qp)),
                                           jnp.where(lane >= HEAD_DIM, qp, jnp.zeros_like(qp))], axis=0))
            return qs

        def score_dot(qs, p, j):
            k0 = pl.multiple_of(j * TK, TK)
            return lax.dot_general(qs[p], k_ref[0, pl.ds(k0, TK), p * LANES:(p + 1) * LANES],
                                   (((1,), (1,)), ((), ())), preferred_element_type=F32)

        def scores(qs, j, slot):
            for p in range(n_pairs):
                z_ref[slot, p] = score_dot(qs, p, j)

        def step(qs, j, slot, masked, prefetch):
            k0 = pl.multiple_of(j * TK, TK)
            hq = TQ // 2
            if masked:
                tri = (lax.broadcasted_iota(jnp.int32, (hq, hq), 1) < lax.broadcasted_iota(jnp.int32, (hq, hq), 0))

            def quadrants(f):
                zero = jnp.zeros((hq, hq), F32)
                rows = []
                for b0 in (0, TQ):
                    rows.append(jnp.concatenate([jnp.where(tri, f(b0, 0), 0.0), zero], axis=1))
                    rows.append(jnp.concatenate([f(b0 + hq, 0), jnp.where(tri, f(b0 + hq, hq), 0.0)], axis=1))
                return jnp.concatenate(rows, axis=0)

            def softplus(p):
                def f(r0, c0):
                    z = z_ref[slot, p, r0:r0 + hq, c0:c0 + hq] if masked else z_ref[slot, p]
                    return jnp.maximum(z, 0.0) + jnp.log(1.0 + jnp.exp2(jnp.abs(z) * -LOG2E))
                return (quadrants(f) if masked else f(0, 0)).astype(BF16)

            def later_dot(sps):
                return jnp.dot(jnp.concatenate(sps, axis=0), u_ref[...], preferred_element_type=F32)

            def weights(p, later):
                lat = later[(p % half) * 2 * TQ:(p % half + 1) * 2 * TQ, :]
                carry = carry_ref[p]
                if masked:
                    att = quadrants(lambda r0, c0: jnp.exp(z_ref[slot, p, r0:r0 + hq, c0:c0 + hq]
                                                           - lat[r0:r0 + hq, c0:c0 + hq] - carry[r0:r0 + hq, :]))
                else:
                    att = jnp.exp(z_ref[slot, p] - lat - jnp.concatenate([carry] * (TK // LANES), axis=1))
                carry = carry + lat[:, 0:1]
                carry_ref[p] = carry
                return att.astype(BF16), jnp.min(carry)

            def value_dot(p, att):
                return jnp.dot(att, v_ref[0, pl.ds(k0, TK), p * LANES:(p + 1) * LANES], preferred_element_type=F32)

            groups = [list(range(g * half, (g + 1) * half)) for g in range(2)]
            later0 = later_dot([softplus(p) for p in groups[0]])
            later1 = later_dot([softplus(p) for p in groups[1]])
            nxt, vals, floors = {}, {}, []
            if prefetch is not None:
                for p in groups[0]:
                    nxt[p] = score_dot(qs, p, prefetch)
            for p in groups[0]:
                att, floor = weights(p, later0)
                floors.append(floor)
                vals[p] = value_dot(p, att)
            if prefetch is not None:
                for p in groups[0]:
                    z_ref[1 - slot, p] = nxt[p]
                for p in groups[1]:
                    nxt[p] = score_dot(qs, p, prefetch)
            for p in groups[1]:
                att, floor = weights(p, later1)
                floors.append(floor)
                vals[p] = value_dot(p, att)
            for p in groups[0]:
                acc_ref[p] += vals[p]
            if prefetch is not None:
                for p in groups[1]:
                    z_ref[1 - slot, p] = nxt[p]
            for p in groups[1]:
                acc_ref[p] += vals[p]
            floor_ref[0] = functools.reduce(jnp.minimum, floors)

        def finish():
            y = jnp.concatenate(
                [jnp.where(lane < HEAD_DIM, acc_ref[p, 0:TQ, :], acc_ref[p, TQ:, :]) for p in range(n_pairs)], axis=1)
            o_ref[0, q_rows, :] = _rms(y, og_ref[...]).astype(BF16)

        def first_block():
            qs = start()
            scores(qs, 0, 0)
            step(qs, 0, 0, True, None)
            finish()

        def later_block():
            qs = start()
            scores(qs, jd, 0)
            step(qs, jd, 0, True, jd - 1)
            step(qs, jd - 1, 1, False, None)
            finish()

            @pl.when(jnp.logical_and(jd >= 2, floor_ref[0] <= SKIP_CARRY))
            def _():
                scores(qs, jd - 2, 1)

                def two_steps(state):
                    j, _ = state
                    step(qs, j, 1, False, jnp.maximum(j - 1, 0))

                    @pl.when(jnp.logical_and(j >= 1, floor_ref[0] <= SKIP_CARRY))
                    def _():
                        step(qs, j - 1, 0, False, jnp.maximum(j - 2, 0))

                    return j - 2, floor_ref[0]

                lax.while_loop(lambda state: jnp.logical_and(state[0] >= 0, state[1] <= SKIP_CARRY),
                               two_steps, (jd - 2, floor_ref[0]))
                finish()

        if sub == 0:
            pl.when(jd == 0)(first_block)
            pl.when(jd > 0)(later_block)
        else:
            later_block()

    for sub in range(Q_PER_STEP):
        one_block(sub)


def _out_ffn_kernel(x_ref, ya_ref, ysb_ref, wout_ref, fg_ref, wgu_ref, wd_ref, o_ref):
    half = ya_ref.shape[1]
    for r0 in range(0, x_ref.shape[0], FFN_ROWS):
        rows = slice(r0, r0 + FFN_ROWS)
        x1 = (x_ref[rows, :]
              + jnp.dot(ya_ref[rows, :], wout_ref[0:half, :], preferred_element_type=F32)
              + jnp.dot(ysb_ref[rows, :], wout_ref[half:, :], preferred_element_type=F32))
        h = _rms(x1, fg_ref[...]).astype(BF16)
        gu = jnp.dot(h, wgu_ref[...], preferred_element_type=F32)
        gate = gu[:, 0:FFN_HIDDEN]
        act = (gate * jax.nn.sigmoid(gate) * gu[:, FFN_HIDDEN:]).astype(BF16)
        o_ref[rows, :] = x1 + jnp.dot(act, wd_ref[...], preferred_element_type=F32)


def _const_spec(shape):
    nd = len(shape)
    return pl.BlockSpec(shape, lambda *_: (0,) * nd, pipeline_mode=pl.Buffered(1))


def _slab_rows(rows, steps):
    n = steps
    while rows % n or (rows // n) % BF16_SUBLANES:
        n //= 2
    return rows // n


def _mix_in(x2, w_in, consts):
    n = x2.shape[0]
    tm = TM_IN
    row = lambda w: pl.BlockSpec((tm, w), lambda i: (i, 0))
    g, rest = consts[0], consts[1:]
    return pl.pallas_call(
        _mix_in_kernel,
        grid=(n // tm,),
        in_specs=[row(D_MODEL), _const_spec(g.shape), _const_spec(w_in.shape)]
        + [_const_spec(c.shape) for c in rest],
        out_specs=[row(CONV_WIDTH + SG_WIDTH), row(SB_WIDTH), row(SB_WIDTH), row(SB_WIDTH)],
        out_shape=[jax.ShapeDtypeStruct((n, CONV_WIDTH + SG_WIDTH), BF16)]
        + [jax.ShapeDtypeStruct((n, SB_WIDTH), BF16)] * 3,
        scratch_shapes=[pltpu.VMEM((HALO + tm, CONV_WIDTH), F32)],
        compiler_params=pltpu.CompilerParams(
            dimension_semantics=("arbitrary",), vmem_limit_bytes=VMEM_LIMIT),
        name="mix_in",
    )(x2, g, w_in, *rest)


def _attn(q, k, v, u, og, casts):
    b, s, w = q.shape
    tq = Q_PER_STEP * TQ
    steps = b * (s // tq)
    w_specs, wb_specs, wb_shapes = [], [], []
    for stack, layer in casts:
        rows, cols = stack.shape[1:]
        slab = _slab_rows(rows, steps)
        every = steps // (rows // slab)
        pick = lambda bi, qi, every=every: (bi * (s // tq) + qi) // every
        w_specs.append(pl.BlockSpec((None, slab, cols), lambda bi, qi, pick=pick, layer=layer: (layer, pick(bi, qi), 0)))
        wb_specs.append(pl.BlockSpec((slab, cols), lambda bi, qi, pick=pick: (pick(bi, qi), 0)))
        wb_shapes.append(jax.ShapeDtypeStruct((rows, cols), BF16))
    out = pl.pallas_call(
        functools.partial(_attn_kernel, n_cast=len(casts)),
        grid=(b, s // tq),
        in_specs=[pl.BlockSpec((1, tq, w), lambda bi, qi: (bi, qi, 0)),
                  pl.BlockSpec((1, s, w), lambda bi, qi: (bi, 0, 0)),
                  pl.BlockSpec((1, s, w), lambda bi, qi: (bi, 0, 0)),
                  _const_spec(u.shape), _const_spec(og.shape)] + w_specs,
        out_specs=[pl.BlockSpec((1, tq, w), lambda bi, qi: (bi, qi, 0))] + wb_specs,
        out_shape=[jax.ShapeDtypeStruct((b, s, w), BF16)] + wb_shapes,
        scratch_shapes=[pltpu.VMEM((SB_HEADS // 2, 2 * TQ, LANES), F32),
                        pltpu.VMEM((SB_HEADS // 2, 2 * TQ, LANES), F32),
                        pltpu.VMEM((2, SB_HEADS // 2, 2 * TQ, TK), F32),
                        pltpu.SMEM((1,), F32)],
        compiler_params=pltpu.CompilerParams(
            dimension_semantics=("arbitrary", "arbitrary"), vmem_limit_bytes=VMEM_LIMIT),
        name="sb_attn",
    )(q, k, v, u, og, *[stack for stack, _ in casts])
    return out[0], out[1:]


def _out_ffn(x2, ya, ysb, wout, fg, wgu, wd):
    n = x2.shape[0]
    tm = TM_FFN
    row = lambda w: pl.BlockSpec((tm, w), lambda i: (i, 0))
    return pl.pallas_call(
        _out_ffn_kernel,
        grid=(n // tm,),
        in_specs=[row(D_MODEL), row(ya.shape[1]), row(ysb.shape[1]),
                  _const_spec(wout.shape), _const_spec(fg.shape), _const_spec(wgu.shape), _const_spec(wd.shape)],
        out_specs=row(D_MODEL),
        out_shape=jax.ShapeDtypeStruct((n, D_MODEL), F32),
        compiler_params=pltpu.CompilerParams(
            dimension_semantics=("parallel",), vmem_limit_bytes=VMEM_LIMIT),
        name="out_ffn",
    )(x2, ya, ysb, wout, fg, wgu, wd)


def kernel(x, mix_norm_g, w_in, conv_w, conv_b, conv_ln_g, conv_ln_b, sg_ln_g, sg_ln_b, sg_w, sg_b,
           q_norm_g, k_norm_g, out_norm_g, w_out, ffn_norm_g, w_gate_up, w_down):
    bsz, seq, d = x.shape
    assert (seq, d) == (SEQ, D_MODEL) and seq % TM_IN == 0 and seq % TK == 0
    assert TQ == TK and TQ // 2 == LANES
    n = bsz * seq
    x2 = x.reshape(n, d)

    hid = jnp.arange(MXU_WIDTH) // HEAD_DIM
    bd = jnp.where(hid[:, None] == hid[None, :], 1.0 / HEAD_DIM, 0.0).astype(BF16)
    ki = jnp.arange(TK)
    u = (ki[:, None] >= ki[None, :]).astype(BF16)
    row2 = lambda a: a.reshape(1, -1)
    w_in_b = w_in[0].astype(BF16)

    for l in range(DEPTH):
        cw = jnp.pad(conv_w[l], ((0, HALO - CONV_KERNEL), (0, 0)))
        sgb = jnp.repeat(sg_b[l].T, HEAD_DIM, axis=1)
        qg = jnp.tile(q_norm_g[l], SB_HEADS).reshape(1, -1) * (HEAD_DIM ** -0.5)
        kg = jnp.tile(k_norm_g[l], SB_HEADS).reshape(1, -1)
        og = row2(out_norm_g[l])
        ya, q, k, v = _mix_in(
            x2, w_in_b,
            [row2(mix_norm_g[l]), cw, row2(conv_b[l]), row2(conv_ln_g[l]), row2(conv_ln_b[l]),
             row2(sg_ln_g[l]), row2(sg_ln_b[l]), sg_w[l], sgb, qg, kg, og[:, :CONV_WIDTH + SG_WIDTH], bd])
        shp = (bsz, seq, SB_WIDTH)
        casts = [(w_out, l), (w_gate_up, l), (w_down, l)] + ([(w_in, l + 1)] if l + 1 < DEPTH else [])
        ysb, cast = _attn(q.reshape(shp), k.reshape(shp), v.reshape(shp), u, og[:, CONV_WIDTH + SG_WIDTH:], casts)
        x2 = _out_ffn(x2, ya, ysb.reshape(n, SB_WIDTH), cast[0], row2(ffn_norm_g[l]), cast[1], cast[2])
        if l + 1 < DEPTH:
            w_in_b = cast[3]
    return x2.reshape(bsz, seq, d)
```

```python
import functools
import math

import jax
import jax.numpy as jnp
from jax import lax
from jax.experimental import pallas as pl
from jax.experimental.pallas import tpu as pltpu

F32 = jnp.float32
BF16 = jnp.bfloat16

D_MODEL = 1024
SEQ = 2048
DEPTH = 4
HEAD_DIM = 64
CONV_WIDTH = 256
SG_WIDTH = 256
SB_WIDTH = 512
SB_HEADS = SB_WIDTH // HEAD_DIM
SG_HEADS = SG_WIDTH // HEAD_DIM
CONV_KERNEL = 31
SG_CHUNK = 128
OFF_SG = 2 * CONV_WIDTH
OFF_SB = OFF_SG + 2 * SG_WIDTH
IN_WIDTH = OFF_SB + 3 * SB_WIDTH
FFN_HIDDEN = 2816
RMS_EPS = 1e-6
LN_EPS = 1e-5
LOG2E = math.log2(math.e)
SKIP_CARRY = 128.0

LANES = 128
SUBLANES = 8
BF16_SUBLANES = 16
MXU_WIDTH = 256
HALO = 32
TM_IN = 1024
TM_FFN = 1024
FFN_ROWS = 512
TQ = 256
Q_PER_STEP = 2
TK = 256
VMEM_LIMIT = 56 * 1024 * 1024


def _rms(x, g):
    return x * lax.rsqrt(jnp.mean(x * x, axis=-1, keepdims=True) + RMS_EPS) * g


def _ln(x, g, b):
    mu = jnp.mean(x, axis=-1, keepdims=True)
    xc = x - mu
    var = jnp.mean(xc * xc, axis=-1, keepdims=True)
    return xc * lax.rsqrt(var + LN_EPS) * g + b


def _mix_in_kernel(x_ref, g_ref, w_in_ref, cw_ref, cb_ref, clg_ref, clb_ref,
                   slg_ref, slb_ref, sgw_ref, sgb_ref, qg_ref, kg_ref, og_ref, bd_ref,
                   ya_ref, q_ref, k_ref, v_ref, hist_ref):
    tm = x_ref.shape[0]
    tiles_per_seq = SEQ // tm
    n_chunks = tm // SG_CHUNK

    @pl.when(pl.program_id(0) % tiles_per_seq == 0)
    def _():
        hist_ref[0:HALO, :] = jnp.zeros((HALO, CONV_WIDTH), F32)

    h = _rms(x_ref[...], g_ref[...]).astype(BF16)

    def project(lo, hi):
        return jnp.dot(h, w_in_ref[:, lo:hi], preferred_element_type=F32)

    pa = project(0, OFF_SG)
    hist_ref[HALO:HALO + tm, :] = pa[:, 0:CONV_WIDTH] * jax.nn.sigmoid(pa[:, CONV_WIDTH:])
    pb = project(OFF_SG, OFF_SB)
    first = HALO - (CONV_KERNEL - 1)
    win = SG_CHUNK + HALO
    for c in range(n_chunks):
        r0 = c * SG_CHUNK
        window = hist_ref[r0:r0 + win, :]
        conv = jnp.zeros((SG_CHUNK, CONV_WIDTH), F32) + cb_ref[...]
        for b in range(SUBLANES):
            taps = [j for j in range(CONV_KERNEL) if (first + j) % SUBLANES == b]
            sh = window if b == 0 else pltpu.roll(window, win - b, 0)
            for j in taps:
                a0 = first + j - b
                conv = conv + sh[a0:a0 + SG_CHUNK, :] * cw_ref[j:j + 1, :]
        yc = _ln(conv, clg_ref[...], clb_ref[...])
        yc = yc * jax.nn.sigmoid(yc)
        ya_ref[r0:r0 + SG_CHUNK, 0:CONV_WIDTH] = _rms(yc, og_ref[:, 0:CONV_WIDTH]).astype(BF16)
    hist_ref[0:HALO, :] = hist_ref[tm:tm + HALO, :]

    pc = project(OFF_SB, IN_WIDTH)
    row = lax.broadcasted_iota(jnp.int32, (SG_CHUNK, SG_CHUNK), 0)
    col = lax.broadcasted_iota(jnp.int32, (SG_CHUNK, SG_CHUNK), 1)
    lane = lax.broadcasted_iota(jnp.int32, (SG_CHUNK, SG_WIDTH), 1)
    ws = [jnp.where(row >= col, sgw_ref[hd], 0.0).astype(BF16) for hd in range(SG_HEADS)]
    for c in range(n_chunks):
        rows = slice(c * SG_CHUNK, (c + 1) * SG_CHUNK)
        uv = pb[rows, :]
        uv = 0.5 * uv * (1.0 + lax.erf(uv * (1.0 / math.sqrt(2.0))))
        vc = _ln(uv[:, SG_WIDTH:], slg_ref[...], slb_ref[...]).astype(BF16)
        mixed = jnp.dot(ws[SG_HEADS - 1], vc, preferred_element_type=F32)
        for hd in range(SG_HEADS - 2, -1, -1):
            mixed = jnp.where(lane < (hd + 1) * HEAD_DIM,
                              jnp.dot(ws[hd], vc, preferred_element_type=F32), mixed)
        ysg = uv[:, 0:SG_WIDTH] * (mixed + sgb_ref[...])
        ya_ref[rows, CONV_WIDTH:] = _rms(ysg, og_ref[:, CONV_WIDTH:]).astype(BF16)

    def head_norm(t, g):
        sq = (t * t).astype(BF16)
        w = bd_ref.shape[0]
        ms = jnp.concatenate([jnp.dot(sq[:, c:c + w], bd_ref[...], preferred_element_type=F32)
                              for c in range(0, SB_WIDTH, w)], axis=1)
        return t * lax.rsqrt(ms + RMS_EPS) * g

    q_ref[...] = head_norm(pc[:, 0:SB_WIDTH], qg_ref[...]).astype(BF16)
    k_ref[...] = head_norm(pc[:, SB_WIDTH:2 * SB_WIDTH], kg_ref[...]).astype(BF16)
    v_ref[...] = pc[:, 2 * SB_WIDTH:].astype(BF16)


def _attn_kernel(q_ref, k_ref, v_ref, u_ref, og_ref, *refs, n_cast):
    w_refs, o_ref, wb_refs = refs[:n_cast], refs[n_cast], refs[n_cast + 1:2 * n_cast + 1]
    acc_ref, carry_ref, z_ref, floor_ref = refs[2 * n_cast + 1:]
    for w_ref, wb_ref in zip(w_refs, wb_refs):
        wb_ref[...] = w_ref[...].astype(BF16)

    n_pairs = SB_HEADS // 2
    half = n_pairs // 2
    lane = lax.broadcasted_iota(jnp.int32, (TQ, LANES), 1)

    def one_block(sub):
        qi = pl.program_id(1) * Q_PER_STEP + sub
        jd = qi // (TK // TQ)
        q_rows = slice(sub * TQ, (sub + 1) * TQ)
        acc, carries, zbuf = acc_ref.at[sub], carry_ref.at[sub], z_ref.at[sub]

        def start():
            acc[...] = jnp.zeros(acc.shape, F32)
            carries[...] = jnp.zeros(carries.shape, F32)
            return stacked_q()

        def stacked_q():
            qs = []
            for p in range(n_pairs):
                qp = q_ref[0, q_rows, p * LANES:(p + 1) * LANES]
                qs.append(jnp.concatenate([jnp.where(lane < HEAD_DIM, qp, jnp.zeros_like(qp)),
                                           jnp.where(lane >= HEAD_DIM, qp, jnp.zeros_like(qp))], axis=0))
            return qs

        def score_dot(qs, p, j):
            k0 = pl.multiple_of(j * TK, TK)
            return lax.dot_general(qs[p], k_ref[0, pl.ds(k0, TK), p * LANES:(p + 1) * LANES],
                                   (((1,), (1,)), ((), ())), preferred_element_type=F32)

        def scores(qs, j, slot):
            for p in range(n_pairs):
                zbuf[slot, p] = score_dot(qs, p, j)

        def step(qs, j, slot, masked, prefetch):
            k0 = pl.multiple_of(j * TK, TK)
            hq = TQ // 2
            if masked:
                tri = (lax.broadcasted_iota(jnp.int32, (hq, hq), 1) < lax.broadcasted_iota(jnp.int32, (hq, hq), 0))

            def quadrants(f):
                zero = jnp.zeros((hq, hq), F32)
                rows = []
                for b0 in (0, TQ):
                    rows.append(jnp.concatenate([jnp.where(tri, f(b0, 0), 0.0), zero], axis=1))
                    rows.append(jnp.concatenate([f(b0 + hq, 0), jnp.where(tri, f(b0 + hq, hq), 0.0)], axis=1))
                return jnp.concatenate(rows, axis=0)

            def softplus(p):
                def f(r0, c0):
                    z = zbuf[slot, p, r0:r0 + hq, c0:c0 + hq] if masked else zbuf[slot, p]
                    return jnp.maximum(z, 0.0) + jnp.log(1.0 + jnp.exp2(jnp.abs(z) * -LOG2E))
                return (quadrants(f) if masked else f(0, 0)).astype(BF16)

            def later_dot(sps):
                return jnp.dot(jnp.concatenate(sps, axis=0), u_ref[...], preferred_element_type=F32)

            def weights(p, later):
                lat = later[(p % half) * 2 * TQ:(p % half + 1) * 2 * TQ, :]
                carry = carries[p]
                if masked:
                    att = quadrants(lambda r0, c0: jnp.exp(zbuf[slot, p, r0:r0 + hq, c0:c0 + hq]
                                                           - lat[r0:r0 + hq, c0:c0 + hq] - carry[r0:r0 + hq, :]))
                else:
                    att = jnp.exp(zbuf[slot, p] - lat - jnp.concatenate([carry] * (TK // LANES), axis=1))
                carry = carry + lat[:, 0:1]
                carries[p] = carry
                return att.astype(BF16), jnp.min(carry)

            def value_dot(p, att):
                return jnp.dot(att, v_ref[0, pl.ds(k0, TK), p * LANES:(p + 1) * LANES], preferred_element_type=F32)

            groups = [list(range(g * half, (g + 1) * half)) for g in range(2)]
            later0 = later_dot([softplus(p) for p in groups[0]])
            later1 = later_dot([softplus(p) for p in groups[1]])
            nxt, vals, floors = {}, {}, []
            if prefetch is not None:
                for p in groups[0]:
                    nxt[p] = score_dot(qs, p, prefetch)
            for p in groups[0]:
                att, floor = weights(p, later0)
                floors.append(floor)
                vals[p] = value_dot(p, att)
            if prefetch is not None:
                for p in groups[0]:
                    zbuf[1 - slot, p] = nxt[p]
                for p in groups[1]:
                    nxt[p] = score_dot(qs, p, prefetch)
            for p in groups[1]:
                att, floor = weights(p, later1)
                floors.append(floor)
                vals[p] = value_dot(p, att)
            for p in groups[0]:
                acc[p] += vals[p]
            if prefetch is not None:
                for p in groups[1]:
                    zbuf[1 - slot, p] = nxt[p]
            for p in groups[1]:
                acc[p] += vals[p]
            floor_ref[sub] = functools.reduce(jnp.minimum, floors)

        def finish():
            y = jnp.concatenate(
                [jnp.where(lane < HEAD_DIM, acc[p, 0:TQ, :], acc[p, TQ:, :]) for p in range(n_pairs)], axis=1)
            o_ref[0, q_rows, :] = _rms(y, og_ref[...]).astype(BF16)

        def first_block():
            qs = start()
            scores(qs, 0, 0)
            step(qs, 0, 0, True, None)
            finish()

        def common():
            qs = start()
            scores(qs, jd, 0)
            step(qs, jd, 0, True, jd - 1)
            step(qs, jd - 1, 1, False, None)
            finish()

        def rest():
            @pl.when(jnp.logical_and(jd >= 2, floor_ref[sub] <= SKIP_CARRY))
            def _():
                qs = stacked_q()
                scores(qs, jd - 2, 1)

                def two_steps(state):
                    j, _ = state
                    step(qs, j, 1, False, jnp.maximum(j - 1, 0))

                    @pl.when(jnp.logical_and(j >= 1, floor_ref[sub] <= SKIP_CARRY))
                    def _():
                        step(qs, j - 1, 0, False, jnp.maximum(j - 2, 0))

                    return j - 2, floor_ref[sub]

                lax.while_loop(lambda state: jnp.logical_and(state[0] >= 0, state[1] <= SKIP_CARRY),
                               two_steps, (jd - 2, floor_ref[sub]))
                finish()

        return first_block, common, rest

    blocks = [one_block(sub) for sub in range(Q_PER_STEP)]

    @pl.when(pl.program_id(1) == 0)
    def _():
        blocks[0][0]()
        for _, common, _ in blocks[1:]:
            common()

    @pl.when(pl.program_id(1) > 0)
    def _():
        for _, common, _ in blocks:
            common()

    for _, _, rest in blocks:
        rest()


def _out_ffn_kernel(x_ref, ya_ref, ysb_ref, wout_ref, fg_ref, wgu_ref, wd_ref, o_ref):
    half = ya_ref.shape[1]
    for r0 in range(0, x_ref.shape[0], FFN_ROWS):
        rows = slice(r0, r0 + FFN_ROWS)
        x1 = (x_ref[rows, :]
              + jnp.dot(ya_ref[rows, :], wout_ref[0:half, :], preferred_element_type=F32)
              + jnp.dot(ysb_ref[rows, :], wout_ref[half:, :], preferred_element_type=F32))
        h = _rms(x1, fg_ref[...]).astype(BF16)
        gu = jnp.dot(h, wgu_ref[...], preferred_element_type=F32)
        gate = gu[:, 0:FFN_HIDDEN]
        act = (gate * jax.nn.sigmoid(gate) * gu[:, FFN_HIDDEN:]).astype(BF16)
        o_ref[rows, :] = x1 + jnp.dot(act, wd_ref[...], preferred_element_type=F32)


def _const_spec(shape):
    nd = len(shape)
    return pl.BlockSpec(shape, lambda *_: (0,) * nd, pipeline_mode=pl.Buffered(1))


def _slab_rows(rows, steps):
    n = steps
    while rows % n or (rows // n) % BF16_SUBLANES:
        n //= 2
    return rows // n


def _mix_in(x2, w_in, consts):
    n = x2.shape[0]
    tm = TM_IN
    row = lambda w: pl.BlockSpec((tm, w), lambda i: (i, 0))
    g, rest = consts[0], consts[1:]
    return pl.pallas_call(
        _mix_in_kernel,
        grid=(n // tm,),
        in_specs=[row(D_MODEL), _const_spec(g.shape), _const_spec(w_in.shape)]
        + [_const_spec(c.shape) for c in rest],
        out_specs=[row(CONV_WIDTH + SG_WIDTH), row(SB_WIDTH), row(SB_WIDTH), row(SB_WIDTH)],
        out_shape=[jax.ShapeDtypeStruct((n, CONV_WIDTH + SG_WIDTH), BF16)]
        + [jax.ShapeDtypeStruct((n, SB_WIDTH), BF16)] * 3,
        scratch_shapes=[pltpu.VMEM((HALO + tm, CONV_WIDTH), F32)],
        compiler_params=pltpu.CompilerParams(
            dimension_semantics=("arbitrary",), vmem_limit_bytes=VMEM_LIMIT),
        name="mix_in",
    )(x2, g, w_in, *rest)


def _attn(q, k, v, u, og, casts):
    b, s, w = q.shape
    tq = Q_PER_STEP * TQ
    steps = b * (s // tq)
    w_specs, wb_specs, wb_shapes = [], [], []
    for stack, layer in casts:
        rows, cols = stack.shape[1:]
        slab = _slab_rows(rows, steps)
        every = steps // (rows // slab)
        pick = lambda bi, qi, every=every: (bi * (s // tq) + qi) // every
        w_specs.append(pl.BlockSpec((None, slab, cols), lambda bi, qi, pick=pick, layer=layer: (layer, pick(bi, qi), 0)))
        wb_specs.append(pl.BlockSpec((slab, cols), lambda bi, qi, pick=pick: (pick(bi, qi), 0)))
        wb_shapes.append(jax.ShapeDtypeStruct((rows, cols), BF16))
    out = pl.pallas_call(
        functools.partial(_attn_kernel, n_cast=len(casts)),
        grid=(b, s // tq),
        in_specs=[pl.BlockSpec((1, tq, w), lambda bi, qi: (bi, qi, 0)),
                  pl.BlockSpec((1, s, w), lambda bi, qi: (bi, 0, 0)),
                  pl.BlockSpec((1, s, w), lambda bi, qi: (bi, 0, 0)),
                  _const_spec(u.shape), _const_spec(og.shape)] + w_specs,
        out_specs=[pl.BlockSpec((1, tq, w), lambda bi, qi: (bi, qi, 0))] + wb_specs,
        out_shape=[jax.ShapeDtypeStruct((b, s, w), BF16)] + wb_shapes,
        scratch_shapes=[pltpu.VMEM((Q_PER_STEP, SB_HEADS // 2, 2 * TQ, LANES), F32),
                        pltpu.VMEM((Q_PER_STEP, SB_HEADS // 2, 2 * TQ, LANES), F32),
                        pltpu.VMEM((Q_PER_STEP, 2, SB_HEADS // 2, 2 * TQ, TK), F32),
                        pltpu.SMEM((Q_PER_STEP,), F32)],
        compiler_params=pltpu.CompilerParams(
            dimension_semantics=("arbitrary", "arbitrary"), vmem_limit_bytes=VMEM_LIMIT),
        name="sb_attn",
    )(q, k, v, u, og, *[stack for stack, _ in casts])
    return out[0], out[1:]


def _out_ffn(x2, ya, ysb, wout, fg, wgu, wd):
    n = x2.shape[0]
    tm = TM_FFN
    row = lambda w: pl.BlockSpec((tm, w), lambda i: (i, 0))
    return pl.pallas_call(
        _out_ffn_kernel,
        grid=(n // tm,),
        in_specs=[row(D_MODEL), row(ya.shape[1]), row(ysb.shape[1]),
                  _const_spec(wout.shape), _const_spec(fg.shape), _const_spec(wgu.shape), _const_spec(wd.shape)],
        out_specs=row(D_MODEL),
        out_shape=jax.ShapeDtypeStruct((n, D_MODEL), F32),
        compiler_params=pltpu.CompilerParams(
            dimension_semantics=("parallel",), vmem_limit_bytes=VMEM_LIMIT),
        name="out_ffn",
    )(x2, ya, ysb, wout, fg, wgu, wd)


def kernel(x, mix_norm_g, w_in, conv_w, conv_b, conv_ln_g, conv_ln_b, sg_ln_g, sg_ln_b, sg_w, sg_b,
           q_norm_g, k_norm_g, out_norm_g, w_out, ffn_norm_g, w_gate_up, w_down):
    bsz, seq, d = x.shape
    assert (seq, d) == (SEQ, D_MODEL) and seq % TM_IN == 0 and seq % TK == 0
    assert TQ == TK and TQ // 2 == LANES
    n = bsz * seq
    x2 = x.reshape(n, d)

    hid = jnp.arange(MXU_WIDTH) // HEAD_DIM
    bd = jnp.where(hid[:, None] == hid[None, :], 1.0 / HEAD_DIM, 0.0).astype(BF16)
    ki = jnp.arange(TK)
    u = (ki[:, None] >= ki[None, :]).astype(BF16)
    row2 = lambda a: a.reshape(1, -1)
    w_in_b = w_in[0].astype(BF16)

    for l in range(DEPTH):
        cw = jnp.pad(conv_w[l], ((0, HALO - CONV_KERNEL), (0, 0)))
        sgb = jnp.repeat(sg_b[l].T, HEAD_DIM, axis=1)
        qg = jnp.tile(q_norm_g[l], SB_HEADS).reshape(1, -1) * (HEAD_DIM ** -0.5)
        kg = jnp.tile(k_norm_g[l], SB_HEADS).reshape(1, -1)
        og = row2(out_norm_g[l])
        ya, q, k, v = _mix_in(
            x2, w_in_b,
            [row2(mix_norm_g[l]), cw, row2(conv_b[l]), row2(conv_ln_g[l]), row2(conv_ln_b[l]),
             row2(sg_ln_g[l]), row2(sg_ln_b[l]), sg_w[l], sgb, qg, kg, og[:, :CONV_WIDTH + SG_WIDTH], bd])
        shp = (bsz, seq, SB_WIDTH)
        casts = [(w_out, l), (w_gate_up, l), (w_down, l)] + ([(w_in, l + 1)] if l + 1 < DEPTH else [])
        ysb, cast = _attn(q.reshape(shp), k.reshape(shp), v.reshape(shp), u, og[:, CONV_WIDTH + SG_WIDTH:], casts)
        x2 = _out_ffn(x2, ya, ysb.reshape(n, SB_WIDTH), cast[0], row2(ffn_norm_g[l]), cast[1], cast[2])
        if l + 1 < DEPTH:
            w_in_b = cast[3]
    return x2.reshape(bsz, seq, d)
```

```python
import functools
import math

import jax
import jax.numpy as jnp
from jax import lax
from jax.experimental import pallas as pl
from jax.experimental.pallas import tpu as pltpu

F32 = jnp.float32
BF16 = jnp.bfloat16

D_MODEL = 1024
SEQ = 2048
DEPTH = 4
HEAD_DIM = 64
CONV_WIDTH = 256
SG_WIDTH = 256
SB_WIDTH = 512
SB_HEADS = SB_WIDTH // HEAD_DIM
SG_HEADS = SG_WIDTH // HEAD_DIM
CONV_KERNEL = 31
SG_CHUNK = 128
OFF_SG = 2 * CONV_WIDTH
OFF_SB = OFF_SG + 2 * SG_WIDTH
IN_WIDTH = OFF_SB + 3 * SB_WIDTH
FFN_HIDDEN = 2816
RMS_EPS = 1e-6
LN_EPS = 1e-5
LOG2E = math.log2(math.e)
SKIP_CARRY = 128.0

LANES = 128
SUBLANES = 8
BF16_SUBLANES = 16
MXU_WIDTH = 256
HALO = 32
TM_IN = 1024
TM_FFN = 1024
FFN_ROWS = 512
FFN_CHUNK = 256
TQ = 256
Q_PER_STEP = 2
TK = 256
VMEM_LIMIT = 56 * 1024 * 1024


def _rms(x, g):
    return x * lax.rsqrt(jnp.mean(x * x, axis=-1, keepdims=True) + RMS_EPS) * g


def _ln(x, g, b):
    mu = jnp.mean(x, axis=-1, keepdims=True)
    xc = x - mu
    var = jnp.mean(xc * xc, axis=-1, keepdims=True)
    return xc * lax.rsqrt(var + LN_EPS) * g + b


def _mix_in_kernel(x_ref, g_ref, w_in_ref, cw_ref, cb_ref, clg_ref, clb_ref,
                   slg_ref, slb_ref, sgw_ref, sgb_ref, qg_ref, kg_ref, og_ref, bd_ref,
                   ya_ref, q_ref, k_ref, v_ref, hist_ref):
    tm = x_ref.shape[0]
    tiles_per_seq = SEQ // tm
    n_chunks = tm // SG_CHUNK

    @pl.when(pl.program_id(0) % tiles_per_seq == 0)
    def _():
        hist_ref[0:HALO, :] = jnp.zeros((HALO, CONV_WIDTH), F32)

    h = _rms(x_ref[...], g_ref[...]).astype(BF16)

    def project(lo, hi):
        return jnp.dot(h, w_in_ref[:, lo:hi], preferred_element_type=F32)

    pa = project(0, OFF_SG)
    hist_ref[HALO:HALO + tm, :] = pa[:, 0:CONV_WIDTH] * jax.nn.sigmoid(pa[:, CONV_WIDTH:])
    pb = project(OFF_SG, OFF_SB)
    first = HALO - (CONV_KERNEL - 1)
    win = SG_CHUNK + HALO
    for c in range(n_chunks):
        r0 = c * SG_CHUNK
        window = hist_ref[r0:r0 + win, :]
        conv = jnp.zeros((SG_CHUNK, CONV_WIDTH), F32) + cb_ref[...]
        for b in range(SUBLANES):
            taps = [j for j in range(CONV_KERNEL) if (first + j) % SUBLANES == b]
            sh = window if b == 0 else pltpu.roll(window, win - b, 0)
            for j in taps:
                a0 = first + j - b
                conv = conv + sh[a0:a0 + SG_CHUNK, :] * cw_ref[j:j + 1, :]
        yc = _ln(conv, clg_ref[...], clb_ref[...])
        yc = yc * jax.nn.sigmoid(yc)
        ya_ref[r0:r0 + SG_CHUNK, 0:CONV_WIDTH] = _rms(yc, og_ref[:, 0:CONV_WIDTH]).astype(BF16)
    hist_ref[0:HALO, :] = hist_ref[tm:tm + HALO, :]

    pc = project(OFF_SB, IN_WIDTH)
    row = lax.broadcasted_iota(jnp.int32, (SG_CHUNK, SG_CHUNK), 0)
    col = lax.broadcasted_iota(jnp.int32, (SG_CHUNK, SG_CHUNK), 1)
    lane = lax.broadcasted_iota(jnp.int32, (SG_CHUNK, SG_WIDTH), 1)
    ws = [jnp.where(row >= col, sgw_ref[hd], 0.0).astype(BF16) for hd in range(SG_HEADS)]
    for c in range(n_chunks):
        rows = slice(c * SG_CHUNK, (c + 1) * SG_CHUNK)
        uv = pb[rows, :]
        uv = 0.5 * uv * (1.0 + lax.erf(uv * (1.0 / math.sqrt(2.0))))
        vc = _ln(uv[:, SG_WIDTH:], slg_ref[...], slb_ref[...]).astype(BF16)
        mixed = jnp.dot(ws[SG_HEADS - 1], vc, preferred_element_type=F32)
        for hd in range(SG_HEADS - 2, -1, -1):
            mixed = jnp.where(lane < (hd + 1) * HEAD_DIM,
                              jnp.dot(ws[hd], vc, preferred_element_type=F32), mixed)
        ysg = uv[:, 0:SG_WIDTH] * (mixed + sgb_ref[...])
        ya_ref[rows, CONV_WIDTH:] = _rms(ysg, og_ref[:, CONV_WIDTH:]).astype(BF16)

    def head_norm(t, g):
        sq = (t * t).astype(BF16)
        w = bd_ref.shape[0]
        ms = jnp.concatenate([jnp.dot(sq[:, c:c + w], bd_ref[...], preferred_element_type=F32)
                              for c in range(0, SB_WIDTH, w)], axis=1)
        return t * lax.rsqrt(ms + RMS_EPS) * g

    q_ref[...] = head_norm(pc[:, 0:SB_WIDTH], qg_ref[...]).astype(BF16)
    k_ref[...] = head_norm(pc[:, SB_WIDTH:2 * SB_WIDTH], kg_ref[...]).astype(BF16)
    v_ref[...] = pc[:, 2 * SB_WIDTH:].astype(BF16)


def _attn_kernel(q_ref, k_ref, v_ref, u_ref, og_ref, *refs, n_cast):
    w_refs, o_ref, wb_refs = refs[:n_cast], refs[n_cast], refs[n_cast + 1:2 * n_cast + 1]
    acc_ref, carry_ref, z_ref, floor_ref = refs[2 * n_cast + 1:]
    for w_ref, wb_ref in zip(w_refs, wb_refs):
        wb_ref[...] = w_ref[...].astype(BF16)

    n_pairs = SB_HEADS // 2
    half = n_pairs // 2
    lane = lax.broadcasted_iota(jnp.int32, (TQ, LANES), 1)

    def one_block(sub):
        qi = pl.program_id(1) * Q_PER_STEP + sub
        jd = qi // (TK // TQ)
        q_rows = slice(sub * TQ, (sub + 1) * TQ)
        acc, carries, zbuf = acc_ref.at[sub], carry_ref.at[sub], z_ref.at[sub]

        def start():
            acc[...] = jnp.zeros(acc.shape, F32)
            carries[...] = jnp.zeros(carries.shape, F32)
            return stacked_q()

        def stacked_q():
            qs = []
            for p in range(n_pairs):
                qp = q_ref[0, q_rows, p * LANES:(p + 1) * LANES]
                qs.append(jnp.concatenate([jnp.where(lane < HEAD_DIM, qp, jnp.zeros_like(qp)),
                                           jnp.where(lane >= HEAD_DIM, qp, jnp.zeros_like(qp))], axis=0))
            return qs

        def score_dot(qs, p, j):
            k0 = pl.multiple_of(j * TK, TK)
            return lax.dot_general(qs[p], k_ref[0, pl.ds(k0, TK), p * LANES:(p + 1) * LANES],
                                   (((1,), (1,)), ((), ())), preferred_element_type=F32)

        def scores(qs, j, slot):
            for p in range(n_pairs):
                zbuf[slot, p] = score_dot(qs, p, j)

        def step(qs, j, slot, masked, prefetch):
            k0 = pl.multiple_of(j * TK, TK)
            hq = TQ // 2
            if masked:
                tri = (lax.broadcasted_iota(jnp.int32, (hq, hq), 1) < lax.broadcasted_iota(jnp.int32, (hq, hq), 0))

            def quadrants(f):
                zero = jnp.zeros((hq, hq), F32)
                rows = []
                for b0 in (0, TQ):
                    rows.append(jnp.concatenate([jnp.where(tri, f(b0, 0), 0.0), zero], axis=1))
                    rows.append(jnp.concatenate([f(b0 + hq, 0), jnp.where(tri, f(b0 + hq, hq), 0.0)], axis=1))
                return jnp.concatenate(rows, axis=0)

            def softplus(p):
                def f(r0, c0):
                    z = zbuf[slot, p, r0:r0 + hq, c0:c0 + hq] if masked else zbuf[slot, p]
                    return jnp.maximum(z, 0.0) + jnp.log(1.0 + jnp.exp2(jnp.abs(z) * -LOG2E))
                return (quadrants(f) if masked else f(0, 0)).astype(BF16)

            def later_dot(sps):
                return jnp.dot(jnp.concatenate(sps, axis=0), u_ref[...], preferred_element_type=F32)

            def weights(p, later):
                lat = later[(p % half) * 2 * TQ:(p % half + 1) * 2 * TQ, :]
                carry = carries[p]
                if masked:
                    att = quadrants(lambda r0, c0: jnp.exp(zbuf[slot, p, r0:r0 + hq, c0:c0 + hq]
                                                           - lat[r0:r0 + hq, c0:c0 + hq] - carry[r0:r0 + hq, :]))
                else:
                    att = jnp.exp(zbuf[slot, p] - lat - jnp.concatenate([carry] * (TK // LANES), axis=1))
                carry = carry + lat[:, 0:1]
                carries[p] = carry
                return att.astype(BF16), jnp.min(carry)

            def value_dot(p, att):
                return jnp.dot(att, v_ref[0, pl.ds(k0, TK), p * LANES:(p + 1) * LANES], preferred_element_type=F32)

            groups = [list(range(g * half, (g + 1) * half)) for g in range(2)]
            later0 = later_dot([softplus(p) for p in groups[0]])
            later1 = later_dot([softplus(p) for p in groups[1]])
            nxt, vals, floors = {}, {}, []
            if prefetch is not None:
                for p in groups[0]:
                    nxt[p] = score_dot(qs, p, prefetch)
            for p in groups[0]:
                att, floor = weights(p, later0)
                floors.append(floor)
                vals[p] = value_dot(p, att)
            if prefetch is not None:
                for p in groups[0]:
                    zbuf[1 - slot, p] = nxt[p]
                for p in groups[1]:
                    nxt[p] = score_dot(qs, p, prefetch)
            for p in groups[1]:
                att, floor = weights(p, later1)
                floors.append(floor)
                vals[p] = value_dot(p, att)
            for p in groups[0]:
                acc[p] += vals[p]
            if prefetch is not None:
                for p in groups[1]:
                    zbuf[1 - slot, p] = nxt[p]
            for p in groups[1]:
                acc[p] += vals[p]
            floor_ref[sub] = functools.reduce(jnp.minimum, floors)

        def finish():
            y = jnp.concatenate(
                [jnp.where(lane < HEAD_DIM, acc[p, 0:TQ, :], acc[p, TQ:, :]) for p in range(n_pairs)], axis=1)
            o_ref[0, q_rows, :] = _rms(y, og_ref[...]).astype(BF16)

        def first_block():
            qs = start()
            scores(qs, 0, 0)
            step(qs, 0, 0, True, None)
            finish()

        def common():
            qs = start()
            scores(qs, jd, 0)
            step(qs, jd, 0, True, jd - 1)
            step(qs, jd - 1, 1, False, None)
            finish()

        def rest():
            @pl.when(jnp.logical_and(jd >= 2, floor_ref[sub] <= SKIP_CARRY))
            def _():
                qs = stacked_q()
                scores(qs, jd - 2, 1)

                def two_steps(state):
                    j, _ = state
                    step(qs, j, 1, False, jnp.maximum(j - 1, 0))

                    @pl.when(jnp.logical_and(j >= 1, floor_ref[sub] <= SKIP_CARRY))
                    def _():
                        step(qs, j - 1, 0, False, jnp.maximum(j - 2, 0))

                    return j - 2, floor_ref[sub]

                lax.while_loop(lambda state: jnp.logical_and(state[0] >= 0, state[1] <= SKIP_CARRY),
                               two_steps, (jd - 2, floor_ref[sub]))
                finish()

        return first_block, common, rest

    blocks = [one_block(sub) for sub in range(Q_PER_STEP)]

    @pl.when(pl.program_id(1) == 0)
    def _():
        blocks[0][0]()
        for _, common, _ in blocks[1:]:
            common()

    @pl.when(pl.program_id(1) > 0)
    def _():
        for _, common, _ in blocks:
            common()

    for _, _, rest in blocks:
        rest()


def _out_ffn_kernel(x_ref, ya_ref, ysb_ref, wout_ref, fg_ref, wgu_ref, wd_ref, o_ref):
    half = ya_ref.shape[1]
    for r0 in range(0, x_ref.shape[0], FFN_ROWS):
        rows = slice(r0, r0 + FFN_ROWS)
        x1 = (x_ref[rows, :]
              + jnp.dot(ya_ref[rows, :], wout_ref[0:half, :], preferred_element_type=F32)
              + jnp.dot(ysb_ref[rows, :], wout_ref[half:, :], preferred_element_type=F32))
        h = _rms(x1, fg_ref[...]).astype(BF16)
        acts = []
        for c0 in range(0, FFN_HIDDEN, FFN_CHUNK):
            gate = jnp.dot(h, wgu_ref[:, c0:c0 + FFN_CHUNK], preferred_element_type=F32)
            up = jnp.dot(h, wgu_ref[:, FFN_HIDDEN + c0:FFN_HIDDEN + c0 + FFN_CHUNK], preferred_element_type=F32)
            acts.append((gate * jax.nn.sigmoid(gate) * up).astype(BF16))
        act = jnp.concatenate(acts, axis=1)
        o_ref[rows, :] = x1 + jnp.dot(act, wd_ref[...], preferred_element_type=F32)


def _const_spec(shape):
    nd = len(shape)
    return pl.BlockSpec(shape, lambda *_: (0,) * nd, pipeline_mode=pl.Buffered(1))


def _slab_rows(rows, steps):
    n = steps
    while rows % n or (rows // n) % BF16_SUBLANES:
        n //= 2
    return rows // n


def _mix_in(x2, w_in, consts):
    n = x2.shape[0]
    tm = TM_IN
    row = lambda w: pl.BlockSpec((tm, w), lambda i: (i, 0))
    g, rest = consts[0], consts[1:]
    return pl.pallas_call(
        _mix_in_kernel,
        grid=(n // tm,),
        in_specs=[row(D_MODEL), _const_spec(g.shape), _const_spec(w_in.shape)]
        + [_const_spec(c.shape) for c in rest],
        out_specs=[row(CONV_WIDTH + SG_WIDTH), row(SB_WIDTH), row(SB_WIDTH), row(SB_WIDTH)],
        out_shape=[jax.ShapeDtypeStruct((n, CONV_WIDTH + SG_WIDTH), BF16)]
        + [jax.ShapeDtypeStruct((n, SB_WIDTH), BF16)] * 3,
        scratch_shapes=[pltpu.VMEM((HALO + tm, CONV_WIDTH), F32)],
        compiler_params=pltpu.CompilerParams(
            dimension_semantics=("arbitrary",), vmem_limit_bytes=VMEM_LIMIT),
        name="mix_in",
    )(x2, g, w_in, *rest)


def _attn(q, k, v, u, og, casts):
    b, s, w = q.shape
    tq = Q_PER_STEP * TQ
    steps = b * (s // tq)
    w_specs, wb_specs, wb_shapes = [], [], []
    for stack, layer in casts:
        rows, cols = stack.shape[1:]
        slab = _slab_rows(rows, steps)
        every = steps // (rows // slab)
        pick = lambda bi, qi, every=every: (bi * (s // tq) + qi) // every
        w_specs.append(pl.BlockSpec((None, slab, cols), lambda bi, qi, pick=pick, layer=layer: (layer, pick(bi, qi), 0)))
        wb_specs.append(pl.BlockSpec((slab, cols), lambda bi, qi, pick=pick: (pick(bi, qi), 0)))
        wb_shapes.append(jax.ShapeDtypeStruct((rows, cols), BF16))
    out = pl.pallas_call(
        functools.partial(_attn_kernel, n_cast=len(casts)),
        grid=(b, s // tq),
        in_specs=[pl.BlockSpec((1, tq, w), lambda bi, qi: (bi, qi, 0)),
                  pl.BlockSpec((1, s, w), lambda bi, qi: (bi, 0, 0)),
                  pl.BlockSpec((1, s, w), lambda bi, qi: (bi, 0, 0)),
                  _const_spec(u.shape), _const_spec(og.shape)] + w_specs,
        out_specs=[pl.BlockSpec((1, tq, w), lambda bi, qi: (bi, qi, 0))] + wb_specs,
        out_shape=[jax.ShapeDtypeStruct((b, s, w), BF16)] + wb_shapes,
        scratch_shapes=[pltpu.VMEM((Q_PER_STEP, SB_HEADS // 2, 2 * TQ, LANES), F32),
                        pltpu.VMEM((Q_PER_STEP, SB_HEADS // 2, 2 * TQ, LANES), F32),
                        pltpu.VMEM((Q_PER_STEP, 2, SB_HEADS // 2, 2 * TQ, TK), F32),
                        pltpu.SMEM((Q_PER_STEP,), F32)],
        compiler_params=pltpu.CompilerParams(
            dimension_semantics=("arbitrary", "arbitrary"), vmem_limit_bytes=VMEM_LIMIT),
        name="sb_attn",
    )(q, k, v, u, og, *[stack for stack, _ in casts])
    return out[0], out[1:]


def _out_ffn(x2, ya, ysb, wout, fg, wgu, wd):
    n = x2.shape[0]
    tm = TM_FFN
    row = lambda w: pl.BlockSpec((tm, w), lambda i: (i, 0))
    return pl.pallas_call(
        _out_ffn_kernel,
        grid=(n // tm,),
        in_specs=[row(D_MODEL), row(ya.shape[1]), row(ysb.shape[1]),
                  _const_spec(wout.shape), _const_spec(fg.shape), _const_spec(wgu.shape), _const_spec(wd.shape)],
        out_specs=row(D_MODEL),
        out_shape=jax.ShapeDtypeStruct((n, D_MODEL), F32),
        compiler_params=pltpu.CompilerParams(
            dimension_semantics=("parallel",), vmem_limit_bytes=VMEM_LIMIT),
        name="out_ffn",
    )(x2, ya, ysb, wout, fg, wgu, wd)


def kernel(x, mix_norm_g, w_in, conv_w, conv_b, conv_ln_g, conv_ln_b, sg_ln_g, sg_ln_b, sg_w, sg_b,
           q_norm_g, k_norm_g, out_norm_g, w_out, ffn_norm_g, w_gate_up, w_down):
    bsz, seq, d = x.shape
    assert (seq, d) == (SEQ, D_MODEL) and seq % TM_IN == 0 and seq % TK == 0
    assert TQ == TK and TQ // 2 == LANES
    n = bsz * seq
    x2 = x.reshape(n, d)

    hid = jnp.arange(MXU_WIDTH) // HEAD_DIM
    bd = jnp.where(hid[:, None] == hid[None, :], 1.0 / HEAD_DIM, 0.0).astype(BF16)
    ki = jnp.arange(TK)
    u = (ki[:, None] >= ki[None, :]).astype(BF16)
    row2 = lambda a: a.reshape(1, -1)
    w_in_b = w_in[0].astype(BF16)

    for l in range(DEPTH):
        cw = jnp.pad(conv_w[l], ((0, HALO - CONV_KERNEL), (0, 0)))
        sgb = jnp.repeat(sg_b[l].T, HEAD_DIM, axis=1)
        qg = jnp.tile(q_norm_g[l], SB_HEADS).reshape(1, -1) * (HEAD_DIM ** -0.5)
        kg = jnp.tile(k_norm_g[l], SB_HEADS).reshape(1, -1)
        og = row2(out_norm_g[l])
        ya, q, k, v = _mix_in(
            x2, w_in_b,
            [row2(mix_norm_g[l]), cw, row2(conv_b[l]), row2(conv_ln_g[l]), row2(conv_ln_b[l]),
             row2(sg_ln_g[l]), row2(sg_ln_b[l]), sg_w[l], sgb, qg, kg, og[:, :CONV_WIDTH + SG_WIDTH], bd])
        shp = (bsz, seq, SB_WIDTH)
        casts = [(w_out, l), (w_gate_up, l), (w_down, l)] + ([(w_in, l + 1)] if l + 1 < DEPTH else [])
        ysb, cast = _attn(q.reshape(shp), k.reshape(shp), v.reshape(shp), u, og[:, CONV_WIDTH + SG_WIDTH:], casts)
        x2 = _out_ffn(x2, ya, ysb.reshape(n, SB_WIDTH), cast[0], row2(ffn_norm_g[l]), cast[1], cast[2])
        if l + 1 < DEPTH:
            w_in_b = cast[3]
    return x2.reshape(bsz, seq, d)
```
